```python
import math
import jax, jax.numpy as jnp
from jax import lax
import numpy as np

D_MODEL = 2048
BATCH = 4
SEQ = 8192
DEPTH = 2
DEC_BATCH = 32
DEC_SEQ = 16
PAST_LEN = 1024

CHUNK = 64
N_EVEN = (DEPTH + 1) // 2
N_ODD = DEPTH // 2
EPS = 1e-6

GDN_NK = 16
GDN_NV = 32
GDN_DK = 128
GDN_DV = 128
GDN_QK_DIM = GDN_NK * GDN_DK
GDN_V_DIM = GDN_NV * GDN_DV
CONV_W = 4
CONV_DIM = 2 * GDN_QK_DIM + GDN_V_DIM
GDN_SPLITS = [CONV_DIM, CONV_DIM + GDN_V_DIM, CONV_DIM + GDN_V_DIM + GDN_NV]
GDN_IN = CONV_DIM + GDN_V_DIM + 2 * GDN_NV

DSA_H = 16
DSA_G = 4
DSA_DH = 128
IDX_H = 16
IDX_D = 64
TOPK_MAX = 256
Q_BLOCK = 128
DSA_Q_DIM = DSA_H * DSA_DH
DSA_KV_DIM = DSA_G * DSA_DH
DSA_SPLITS = [DSA_Q_DIM, DSA_Q_DIM + DSA_KV_DIM, DSA_Q_DIM + 2 * DSA_KV_DIM,
              DSA_Q_DIM + 2 * DSA_KV_DIM + IDX_H * IDX_D,
              DSA_Q_DIM + 2 * DSA_KV_DIM + IDX_H * IDX_D + IDX_D]
DSA_IN = DSA_SPLITS[-1] + IDX_H

NUM_BUCKETS = 32
MAX_DISTANCE = 128

D_FF = 5632
N_EXPERTS = 8
TOP_K = 2
D_FF_EXPERT = 7168
EXPERT_BLOCK = 512

PLE_DIM = 256

kernel_name = 'chunk_streaming_gdn_dsa_hybrid_step'


def rmsnorm(x, g):
    xf = x.astype(jnp.float32)
    y = xf * lax.rsqrt(jnp.mean(xf * xf, axis=-1, keepdims=True) + EPS)
    return (y * g.astype(jnp.float32)).astype(x.dtype)


def l2norm(x):
    xf = x.astype(jnp.float32)
    return xf * lax.rsqrt(jnp.sum(xf * xf, axis=-1, keepdims=True) + EPS)


def swiglu(x, wg, wu, wd):
    return (jax.nn.silu(x @ wg) * (x @ wu)) @ wd


def t5_bucket(rel):
    half = NUM_BUCKETS // 2
    exact = half // 2
    base = jnp.where(rel > 0, half, 0)
    n = jnp.abs(rel)
    nf = jnp.maximum(n, 1).astype(jnp.float32)
    large = exact + (jnp.log(nf / exact) / math.log(MAX_DISTANCE / exact) * (half - exact)).astype(jnp.int32)
    large = jnp.minimum(large, half - 1)
    return base + jnp.where(n < exact, n, large)


def gated_delta_rule(q, k, v, g, beta, s0):
    B, T, H, _ = q.shape
    DV = v.shape[-1]
    C = min(CHUNK, T)
    N = T // C

    def to_chunks(a):
        a = a.reshape((B, N, C, H) + a.shape[3:])
        return jnp.swapaxes(jnp.moveaxis(a, 1, 0), 2, 3)

    incl = jnp.tril(jnp.ones((C, C), bool))
    strict = jnp.tril(jnp.ones((C, C), bool), -1)
    eye = jnp.eye(C, dtype=jnp.float32)

    def step(S, inp):
        qc, kc, vc, gc, bc = inp
        gcum = jnp.cumsum(gc, axis=-1)
        decay = jnp.exp(jnp.where(incl, gcum[..., :, None] - gcum[..., None, :], -jnp.inf))
        kb = kc * bc[..., None]
        tmat = eye + jnp.where(strict, jnp.einsum('bhid,bhjd->bhij', kb, kc) * decay, 0.0)
        rhs = jnp.concatenate([vc * bc[..., None], kb * jnp.exp(gcum)[..., None]], axis=-1)
        sol = lax.linalg.triangular_solve(tmat, rhs, left_side=True, lower=True)
        u, w = sol[..., :DV], sol[..., DV:]
        v_new = u - jnp.einsum('bhck,bhkv->bhcv', w, S)
        qk = jnp.einsum('bhid,bhjd->bhij', qc, kc) * decay
        o = (jnp.einsum('bhck,bhkv->bhcv', qc * jnp.exp(gcum)[..., None], S)
             + jnp.einsum('bhij,bhjv->bhiv', qk, v_new))
        g_last = gcum[..., -1:]
        S = (S * jnp.exp(g_last)[..., None]
             + jnp.einsum('bhck,bhcv->bhkv', kc * jnp.exp(g_last - gcum)[..., None], v_new))
        return S, o

    S, o = lax.scan(step, s0, (to_chunks(q), to_chunks(k), to_chunks(v), to_chunks(g), to_chunks(beta)))
    o = jnp.swapaxes(jnp.moveaxis(o, 0, 1), 2, 3).reshape(B, T, H, DV)
    return o, S


def gdn_mixer(xn, conv_buf, s0, w_in, conv_w, a_log, dt_bias, g_onorm, w_out):
    B, T, _ = xn.shape
    qkv, z, b, a = jnp.split(xn @ w_in, GDN_SPLITS, axis=-1)
    xpad = jnp.concatenate([conv_buf.astype(qkv.dtype), qkv], axis=1)
    new_buf = xpad[:, -(CONV_W - 1):]
    conv = lax.conv_general_dilated(xpad, conv_w[:, None, :].astype(qkv.dtype), (1,), 'VALID',
                                    dimension_numbers=('NWC', 'WIO', 'NWC'),
                                    feature_group_count=CONV_DIM)
    conv = jax.nn.silu(conv)
    q, k, v = jnp.split(conv, [GDN_QK_DIM, 2 * GDN_QK_DIM], axis=-1)
    rep = GDN_NV // GDN_NK
    q = jnp.repeat(l2norm(q.reshape(B, T, GDN_NK, GDN_DK)) * GDN_DK ** -0.5, rep, axis=2)
    k = jnp.repeat(l2norm(k.reshape(B, T, GDN_NK, GDN_DK)), rep, axis=2)
    v = v.reshape(B, T, GDN_NV, GDN_DV).astype(jnp.float32)
    beta = jax.nn.sigmoid(b.astype(jnp.float32))
    g = -jnp.exp(a_log.astype(jnp.float32)) * jax.nn.softplus(a.astype(jnp.float32) + dt_bias.astype(jnp.float32))
    o, s_final = gated_delta_rule(q, k, v, g, beta, s0.astype(jnp.float32))
    o = rmsnorm(o, g_onorm) * jax.nn.silu(z.reshape(B, T, GDN_NV, GDN_DV).astype(jnp.float32))
    return o.reshape(B, T, GDN_V_DIM).astype(xn.dtype) @ w_out, new_buf, s_final


def dsa_project(xn, w_in, g_q, g_k, g_ki):
    B, T, _ = xn.shape
    q, k, v, qi, ki, wi = jnp.split(xn @ w_in, DSA_SPLITS, axis=-1)
    q = rmsnorm(q.reshape(B, T, DSA_H, DSA_DH), g_q)
    k = rmsnorm(k.reshape(B, T, DSA_G, DSA_DH), g_k)
    v = v.reshape(B, T, DSA_G, DSA_DH)
    qi = qi.reshape(B, T, IDX_H, IDX_D)
    ki = rmsnorm(ki, g_ki)
    wi = wi * IDX_H ** -0.5
    return q, k, v, qi, ki, wi


def index_scores(qi, wi, ki):
    s = jnp.einsum('bqhd,bsd->bqhs', qi, ki).astype(jnp.float32) * IDX_D ** -0.5
    return jnp.einsum('bqh,bqhs->bqs', wi.astype(jnp.float32), jax.nn.relu(s))


def take_rows(rows, idx):
    return jax.vmap(lambda r, i: r[i])(rows, idx)


def sparse_attend(q, kg, vg, qpos, kpos, valid, rel_bias):
    B, Q, H, dh = q.shape
    K, G = kg.shape[2], kg.shape[3]
    R = H // G
    qg = q.reshape(B, Q, G, R, dh)
    logits = jnp.einsum('bqgrd,bqkgd->bqgrk', qg, kg).astype(jnp.float32) * dh ** -0.5
    bias = rel_bias.astype(jnp.float32)[t5_bucket(kpos - qpos[None, :, None])]
    bias = jnp.moveaxis(bias.reshape(B, Q, K, G, R), 2, 4)
    logits = jnp.where(valid[:, :, None, None, :], logits + bias, -jnp.inf)
    probs = jax.nn.softmax(logits, axis=-1).astype(vg.dtype)
    return jnp.einsum('bqgrk,bqkgd->bqgrd', probs, vg).reshape(B, Q, H, dh)


def dsa_prompt(q, k, v, qi, ki, wi, rel_bias):
    B, T = q.shape[0], q.shape[1]
    topk = min(TOPK_MAX, T // 4)
    nb = T // Q_BLOCK
    key_pos = jnp.arange(T, dtype=jnp.int32)

    def blk(args):
        qb, qib, wib, bi = args
        qpos = bi * Q_BLOCK + jnp.arange(Q_BLOCK, dtype=jnp.int32)
        limit = (qpos // CHUNK + 1) * CHUNK
        sc = index_scores(qib, wib, ki)
        sc = jnp.where(key_pos[None, None, :] < limit[None, :, None], sc, -jnp.inf)
        _, idx = lax.top_k(sc, topk)
        valid = idx < limit[None, :, None]
        return sparse_attend(qb, take_rows(k, idx), take_rows(v, idx), qpos, idx, valid, rel_bias)

    def to_blocks(a):
        return jnp.moveaxis(a.reshape((B, nb, Q_BLOCK) + a.shape[2:]), 1, 0)

    o = lax.map(blk, (to_blocks(q), to_blocks(qi), to_blocks(wi), jnp.arange(nb, dtype=jnp.int32)))
    return jnp.moveaxis(o, 0, 1).reshape(B, T, DSA_H, DSA_DH)


def dsa_sample(q, k, v, qi, ki, wi, ck, cv, cki, rel_bias):
    past = ck.shape[1]
    k_all = jnp.concatenate([ck.astype(k.dtype), k], axis=1)
    v_all = jnp.concatenate([cv.astype(v.dtype), v], axis=1)
    ki_all = jnp.concatenate([cki.astype(ki.dtype), ki], axis=1)
    topk = min(TOPK_MAX, k_all.shape[1] // 4)
    qpos = past + jnp.arange(q.shape[1], dtype=jnp.int32)
    _, idx = lax.top_k(index_scores(qi, wi, ki_all), topk)
    valid = jnp.ones(idx.shape, bool)
    return sparse_attend(q, take_rows(k_all, idx), take_rows(v_all, idx), qpos, idx, valid, rel_bias)


def moe_swiglu(x, w_router, w_gate, w_up, w_down):
    B, T, D = x.shape
    xf = x.reshape(B * T, D)
    N = B * T
    A = N * TOP_K
    top_val, top_idx = lax.top_k((xf @ w_router).astype(jnp.float32), TOP_K)
    gates = jax.nn.softmax(top_val, axis=-1).astype(x.dtype)
    flat_e = top_idx.reshape(-1)
    order = jnp.argsort(flat_e)
    e_sorted = flat_e[order]
    counts = jnp.bincount(flat_e, length=N_EXPERTS)
    padded = (counts + EXPERT_BLOCK - 1) // EXPERT_BLOCK * EXPERT_BLOCK
    pad_end = jnp.cumsum(padded)
    pad_start = pad_end - padded
    grp_start = jnp.cumsum(counts) - counts
    dest_sorted = pad_start[e_sorted] + jnp.arange(A, dtype=jnp.int32) - grp_start[e_sorted]
    dest = jnp.zeros((A,), jnp.int32).at[order].set(dest_sorted.astype(jnp.int32))
    n_blocks = (A + N_EXPERTS * (EXPERT_BLOCK - 1)) // EXPERT_BLOCK
    P = n_blocks * EXPERT_BLOCK
    row_src = jnp.full((P,), N, jnp.int32).at[dest].set(jnp.arange(A, dtype=jnp.int32) // TOP_K)
    block_e = jnp.minimum(jnp.searchsorted(pad_end, jnp.arange(n_blocks, dtype=jnp.int32) * EXPERT_BLOCK,
                                           side='right'), N_EXPERTS - 1)
    xpad = jnp.concatenate([xf, jnp.zeros((1, D), xf.dtype)], axis=0)[row_src].reshape(n_blocks, EXPERT_BLOCK, D)

    def expert_block(args):
        xb, e = args
        return swiglu(xb, w_gate[e], w_up[e], w_down[e])

    yp = lax.map(expert_block, (xpad, block_e)).reshape(P, D)
    y = jnp.sum(yp[dest].reshape(N, TOP_K, D) * gates[..., None], axis=1)
    return y.reshape(B, T, D)


def ple_add(h, p, g, w_in, w_gate):
    gate = jax.nn.sigmoid(rmsnorm(h, g) @ w_gate)
    return h + gate * (p.astype(h.dtype) @ w_in)


def setup_inputs(seed: int = 0) -> dict:
    key = jax.random.key(seed)
    ks = iter(jax.random.split(key, 48))
    f32 = jnp.float32

    def nrm(shape, scale):
        return jax.random.normal(next(ks), shape, f32) * scale

    def gain(shape):
        return 1.0 + nrm(shape, 0.02)

    return {
        'x_prompt': nrm((BATCH, SEQ, D_MODEL), 1.0),
        'x_sample': nrm((DEC_BATCH, DEC_SEQ, D_MODEL), 1.0),
        'state_gdn': nrm((N_EVEN, DEC_BATCH, GDN_NV, GDN_DK, GDN_DV), 0.1),
        'state_gdn_conv': nrm((N_EVEN, DEC_BATCH, CONV_W - 1, CONV_DIM), 1.0),
        'cache_dsa_k': nrm((N_ODD, DEC_BATCH, PAST_LEN, DSA_G, DSA_DH), 1.0),
        'cache_dsa_v': nrm((N_ODD, DEC_BATCH, PAST_LEN, DSA_G, DSA_DH), 1.0),
        'cache_dsa_kidx': nrm((N_ODD, DEC_BATCH, PAST_LEN, IDX_D), 1.0),
        'p_prompt': nrm((DEPTH, BATCH, SEQ, PLE_DIM), 1.0),
        'p_sample': nrm((DEPTH, DEC_BATCH, DEC_SEQ, PLE_DIM), 1.0),
        'g_mix': gain((DEPTH, D_MODEL)),
        'g_ffn': gain((DEPTH, D_MODEL)),
        'g_ple': gain((DEPTH, D_MODEL)),
        'w_ple_in': nrm((DEPTH, PLE_DIM, D_MODEL), PLE_DIM ** -0.5),
        'w_ple_gate': nrm((DEPTH, D_MODEL, D_MODEL), D_MODEL ** -0.5),
        'w_in_gdn': nrm((N_EVEN, D_MODEL, GDN_IN), D_MODEL ** -0.5),
        'conv_w_gdn': nrm((N_EVEN, CONV_W, CONV_DIM), CONV_W ** -0.5),
        'a_log_gdn': jnp.log(jax.random.uniform(next(ks), (N_EVEN, GDN_NV), f32, 1.0, 16.0)),
        'dt_bias_gdn': nrm((N_EVEN, GDN_NV), 0.1),
        'g_onorm_gdn': gain((N_EVEN, GDN_DV)),
        'w_out_gdn': nrm((N_EVEN, GDN_V_DIM, D_MODEL), GDN_V_DIM ** -0.5),
        'w_in_dsa': nrm((N_ODD, D_MODEL, DSA_IN), D_MODEL ** -0.5),
        'g_qnorm_dsa': gain((N_ODD, DSA_DH)),
        'g_knorm_dsa': gain((N_ODD, DSA_DH)),
        'g_kidx_norm_dsa': gain((N_ODD, IDX_D)),
        'w_out_dsa': nrm((N_ODD, DSA_Q_DIM, D_MODEL), DSA_Q_DIM ** -0.5),
        'rel_bias': nrm((NUM_BUCKETS, DSA_H), 0.5),
        'w_gate_ffn': nrm((N_EVEN, D_MODEL, D_FF), D_MODEL ** -0.5),
        'w_up_ffn': nrm((N_EVEN, D_MODEL, D_FF), D_MODEL ** -0.5),
        'w_down_ffn': nrm((N_EVEN, D_FF, D_MODEL), D_FF ** -0.5),
        'w_router': nrm((N_ODD, D_MODEL, N_EXPERTS), D_MODEL ** -0.5),
        'w_gate_moe': nrm((N_ODD, N_EXPERTS, D_MODEL, D_FF_EXPERT), D_MODEL ** -0.5),
        'w_up_moe': nrm((N_ODD, N_EXPERTS, D_MODEL, D_FF_EXPERT), D_MODEL ** -0.5),
        'w_down_moe': nrm((N_ODD, N_EXPERTS, D_FF_EXPERT, D_MODEL), D_FF_EXPERT ** -0.5),
    }


def reference(x_prompt, x_sample, state_gdn, state_gdn_conv, cache_dsa_k, cache_dsa_v, cache_dsa_kidx,
              p_prompt, p_sample, g_mix, g_ffn, g_ple, w_ple_in, w_ple_gate,
              w_in_gdn, conv_w_gdn, a_log_gdn, dt_bias_gdn, g_onorm_gdn, w_out_gdn,
              w_in_dsa, g_qnorm_dsa, g_knorm_dsa, g_kidx_norm_dsa, w_out_dsa, rel_bias,
              w_gate_ffn, w_up_ffn, w_down_ffn, w_router, w_gate_moe, w_up_moe, w_down_moe):
    hp, hs = x_prompt, x_sample
    Bp, Tp = hp.shape[0], hp.shape[1]
    Bs, Ts = hs.shape[0], hs.shape[1]
    gdn_s_p, gdn_c_p, gdn_s_s, gdn_c_s = [], [], [], []
    dk_p, dv_p, dki_p, dk_s, dv_s, dki_s = [], [], [], [], [], []
    for i in range(DEPTH):
        j = i // 2
        if i % 2 == 0:
            wa = (w_in_gdn[j], conv_w_gdn[j], a_log_gdn[j], dt_bias_gdn[j], g_onorm_gdn[j], w_out_gdn[j])
            zero_buf = jnp.zeros((Bp, CONV_W - 1, CONV_DIM), hp.dtype)
            zero_state = jnp.zeros((Bp, GDN_NV, GDN_DK, GDN_DV), jnp.float32)
            mp, cbp, sp = gdn_mixer(rmsnorm(hp, g_mix[i]), zero_buf, zero_state, *wa)
            ms, cbs, ss = gdn_mixer(rmsnorm(hs, g_mix[i]), state_gdn_conv[j], state_gdn[j], *wa)
            hp = hp + mp
            hs = hs + ms
            gdn_s_p.append(sp); gdn_c_p.append(cbp); gdn_s_s.append(ss); gdn_c_s.append(cbs)
            hp = hp + swiglu(rmsnorm(hp, g_ffn[i]), w_gate_ffn[j], w_up_ffn[j], w_down_ffn[j])
            hs = hs + swiglu(rmsnorm(hs, g_ffn[i]), w_gate_ffn[j], w_up_ffn[j], w_down_ffn[j])
        else:
            wb = (w_in_dsa[j], g_qnorm_dsa[j], g_knorm_dsa[j], g_kidx_norm_dsa[j])
            q, k, v, qi, ki, wi = dsa_project(rmsnorm(hp, g_mix[i]), *wb)
            att = dsa_prompt(q, k, v, qi, ki, wi, rel_bias)
            hp = hp + att.reshape(Bp, Tp, DSA_Q_DIM) @ w_out_dsa[j]
            dk_p.append(k); dv_p.append(v); dki_p.append(ki)
            q, k, v, qi, ki, wi = dsa_project(rmsnorm(hs, g_mix[i]), *wb)
            att = dsa_sample(q, k, v, qi, ki, wi, cache_dsa_k[j], cache_dsa_v[j], cache_dsa_kidx[j], rel_bias)
            hs = hs + att.reshape(Bs, Ts, DSA_Q_DIM) @ w_out_dsa[j]
            dk_s.append(k); dv_s.append(v); dki_s.append(ki)
            hp = hp + moe_swiglu(rmsnorm(hp, g_ffn[i]), w_router[j], w_gate_moe[j], w_up_moe[j], w_down_moe[j])
            hs = hs + moe_swiglu(rmsnorm(hs, g_ffn[i]), w_router[j], w_gate_moe[j], w_up_moe[j], w_down_moe[j])
        hp = ple_add(hp, p_prompt[i], g_ple[i], w_ple_in[i], w_ple_gate[i])
        hs = ple_add(hs, p_sample[i], g_ple[i], w_ple_in[i], w_ple_gate[i])
    return (hp, hs,
            jnp.stack(gdn_s_p), jnp.stack(gdn_c_p), jnp.stack(dk_p), jnp.stack(dv_p), jnp.stack(dki_p),
            jnp.stack(gdn_s_s), jnp.stack(gdn_c_s), jnp.stack(dk_s), jnp.stack(dv_s), jnp.stack(dki_s))
```

```python
import functools
import math

import numpy as np
import jax
import jax.numpy as jnp
from jax import lax
from jax.experimental import pallas as pl
from jax.experimental.pallas import tpu as pltpu

F32 = jnp.float32
BF16 = jnp.bfloat16
I32 = jnp.int32

EPS = 1e-6
CHUNK = 64
TOPK_MAX = 256
MOE_TOP_K = 2
NUM_BUCKETS = 32
MAX_DISTANCE = 128
NEG = -1e30
INT_MIN = -(2 ** 31)
HI = lax.Precision.HIGHEST

V7X_VMEM_LIMIT = 56 * 1024 * 1024
LANES = 128


def _cparams(*sem):
    return pltpu.CompilerParams(dimension_semantics=sem, vmem_limit_bytes=V7X_VMEM_LIMIT)


def _sigmoid(x):
    return 1.0 / (1.0 + jnp.exp(-x))


def _silu(x):
    return x * _sigmoid(x)


def _rms(x, g):
    return x * lax.rsqrt(jnp.mean(x * x, axis=-1, keepdims=True) + EPS) * g


def _dot(a, b):
    return jnp.dot(a, b, preferred_element_type=F32)


def _dot_nt(a, b):
    return lax.dot_general(a, b, (((1,), (1,)), ((), ())), preferred_element_type=F32)


def _dot_tn(a, b):
    return lax.dot_general(a, b, (((0,), (0,)), ((), ())), preferred_element_type=F32)


def _dot_hi(a, b):
    return jnp.dot(a, b, preferred_element_type=F32, precision=HI)


def _pick_tile(n, pref):
    t = min(n, pref)
    while n % t:
        t //= 2
    return t


def _mm_kernel(*refs, has_gain, n_w, n_extra, n_out, epilogue):
    x_ref = refs[0]
    pos = 1
    g_ref = None
    if has_gain:
        g_ref = refs[pos]
        pos += 1
    w_refs = refs[pos:pos + n_w]
    pos += n_w
    e_refs = refs[pos:pos + n_extra]
    pos += n_extra
    o_refs = refs[pos:pos + n_out]
    pos += n_out
    xs_ref = refs[pos]

    @pl.when(pl.program_id(1) == 0)
    def _():
        x = x_ref[...].astype(F32)
        if has_gain:
            x = _rms(x, g_ref[...])
        xs_ref[...] = x.astype(BF16)

    xs = xs_ref[...]
    accs = [_dot(xs, w[...]) for w in w_refs]
    outs = epilogue(accs, [e[...] for e in e_refs], pl.program_id(1))
    for o_ref, o in zip(o_refs, outs):
        o_ref[...] = o.astype(o_ref.dtype)


def _mm(x, ws, epilogue, out_dtypes, *, gain=None, extras=(), tm=512, tn=512):
    M, K = x.shape
    N = ws[0].shape[1]
    tm = _pick_tile(M, tm)
    tn = _pick_tile(N, tn)
    in_specs = [pl.BlockSpec((tm, K), lambda i, j: (i, 0))]
    args = [x]
    if gain is not None:
        in_specs.append(pl.BlockSpec((1, K), lambda i, j: (0, 0)))
        args.append(gain.reshape(1, K).astype(F32))
    for w in ws:
        in_specs.append(pl.BlockSpec((K, tn), lambda i, j: (0, j)))
        args.append(w)
    for e in extras:
        if e.shape[0] == 1:
            in_specs.append(pl.BlockSpec((1, tn), lambda i, j: (0, j)))
        else:
            in_specs.append(pl.BlockSpec((tm, tn), lambda i, j: (i, j)))
        args.append(e)
    out_shape = [jax.ShapeDtypeStruct((M, N), d) for d in out_dtypes]
    out_specs = [pl.BlockSpec((tm, tn), lambda i, j: (i, j)) for _ in out_dtypes]
    kern = functools.partial(_mm_kernel, has_gain=gain is not None, n_w=len(ws), n_extra=len(extras),
                             n_out=len(out_dtypes), epilogue=epilogue)
    return pl.pallas_call(
        kern,
        grid=(M // tm, N // tn),
        in_specs=in_specs,
        out_specs=out_specs,
        out_shape=out_shape,
        scratch_shapes=[pltpu.VMEM((tm, K), BF16)],
        compiler_params=_cparams("parallel", "arbitrary"),
    )(*args)


def _ep_plain(accs, extras, j):
    return [accs[0]]


def _ep_residual(accs, extras, j):
    return [extras[0] + accs[0]]


def _ep_headnorm(accs, extras, j, *, dh, n_norm_tiles):
    a = accs[0]
    g = extras[0]
    cols = []
    for h in range(a.shape[1] // dh):
        sl = slice(h * dh, (h + 1) * dh)
        cols.append(_rms(a[:, sl], g[:, sl]))
    normed = jnp.concatenate(cols, axis=1) if len(cols) > 1 else cols[0]
    return [jnp.where(j < n_norm_tiles, normed, a)]


def _ffn_kernel(*refs, has_gain, has_be):
    pos = 0
    if has_be:
        pos = 1
    x_ref = refs[pos]
    pos += 1
    g_ref = None
    if has_gain:
        g_ref = refs[pos]
        pos += 1
    wg_ref, wu_ref, wd_ref, o_ref, xs_ref, acc_ref = refs[pos:pos + 6]
    j = pl.program_id(1)

    @pl.when(j == 0)
    def _():
        if has_gain:
            x = x_ref[...].astype(F32)
            xs_ref[...] = _rms(x, g_ref[...]).astype(BF16)
            acc_ref[...] = x
        else:
            xs_ref[...] = x_ref[...].astype(BF16)
            acc_ref[...] = jnp.zeros_like(acc_ref)

    xs = xs_ref[...]
    wg = wg_ref[...].reshape(wg_ref.shape[-2:])
    wu = wu_ref[...].reshape(wu_ref.shape[-2:])
    wd = wd_ref[...].reshape(wd_ref.shape[-2:])
    a = (_silu(_dot(xs, wg)) * _dot(xs, wu)).astype(BF16)
    acc_ref[...] += _dot(a, wd)

    @pl.when(j == pl.num_programs(1) - 1)
    def _():
        o_ref[...] = acc_ref[...].astype(o_ref.dtype)


def _ffn_dense(h, gain, wg, wu, wd, *, tm=512, tf=512):
    M, D = h.shape
    Fdim = wg.shape[1]
    tm = _pick_tile(M, tm)
    tf = _pick_tile(Fdim, tf)
    return pl.pallas_call(
        functools.partial(_ffn_kernel, has_gain=True, has_be=False),
        grid=(M // tm, Fdim // tf),
        in_specs=[pl.BlockSpec((tm, D), lambda i, j: (i, 0)),
                  pl.BlockSpec((1, D), lambda i, j: (0, 0)),
                  pl.BlockSpec((D, tf), lambda i, j: (0, j)),
                  pl.BlockSpec((D, tf), lambda i, j: (0, j)),
                  pl.BlockSpec((tf, D), lambda i, j: (j, 0))],
        out_specs=pl.BlockSpec((tm, D), lambda i, j: (i, 0)),
        out_shape=jax.ShapeDtypeStruct((M, D), F32),
        scratch_shapes=[pltpu.VMEM((tm, D), BF16), pltpu.VMEM((tm, D), F32)],
        compiler_params=_cparams("parallel", "arbitrary"),
    )(h, gain.reshape(1, D).astype(F32), wg, wu, wd)


def _ffn_experts(xs, block_e, wg, wu, wd, *, bm, tf=512):
    P, D = xs.shape
    Fdim = wg.shape[2]
    tf = _pick_tile(Fdim, tf)
    grid_spec = pltpu.PrefetchScalarGridSpec(
        num_scalar_prefetch=1,
        grid=(P // bm, Fdim // tf),
        in_specs=[pl.BlockSpec((bm, D), lambda i, j, be: (i, 0)),
                  pl.BlockSpec((1, D, tf), lambda i, j, be: (be[i], 0, j)),
                  pl.BlockSpec((1, D, tf), lambda i, j, be: (be[i], 0, j)),
                  pl.BlockSpec((1, tf, D), lambda i, j, be: (be[i], j, 0))],
        out_specs=pl.BlockSpec((bm, D), lambda i, j, be: (i, 0)),
        scratch_shapes=[pltpu.VMEM((bm, D), BF16), pltpu.VMEM((bm, D), F32)],
    )
    return pl.pallas_call(
        functools.partial(_ffn_kernel, has_gain=False, has_be=True),
        grid_spec=grid_spec,
        out_shape=jax.ShapeDtypeStruct((P, D), F32),
        compiler_params=_cparams("parallel", "arbitrary"),
    )(block_e, xs, wg, wu, wd)


def _ple_kernel(h_ref, g_ref, wgate_ref, p_ref, wp_ref, ht_ref, o_ref, xs_ref):
    @pl.when(pl.program_id(1) == 0)
    def _():
        xs_ref[...] = _rms(h_ref[...], g_ref[...]).astype(BF16)

    gate = _sigmoid(_dot(xs_ref[...], wgate_ref[...]))
    o_ref[...] = ht_ref[...] + gate * _dot(p_ref[...].astype(BF16), wp_ref[...])


def _ple_add(h, p, gain, wp, wgate, *, tm=512, tn=512):
    M, D = h.shape
    Pd = p.shape[1]
    tm = _pick_tile(M, tm)
    tn = _pick_tile(D, tn)
    return pl.pallas_call(
        _ple_kernel,
        grid=(M // tm, D // tn),
        in_specs=[pl.BlockSpec((tm, D), lambda i, j: (i, 0)),
                  pl.BlockSpec((1, D), lambda i, j: (0, 0)),
                  pl.BlockSpec((D, tn), lambda i, j: (0, j)),
                  pl.BlockSpec((tm, Pd), lambda i, j: (i, 0)),
                  pl.BlockSpec((Pd, tn), lambda i, j: (0, j)),
                  pl.BlockSpec((tm, tn), lambda i, j: (i, j))],
        out_specs=pl.BlockSpec((tm, tn), lambda i, j: (i, j)),
        out_shape=jax.ShapeDtypeStruct((M, D), F32),
        scratch_shapes=[pltpu.VMEM((tm, D), BF16)],
        compiler_params=_cparams("parallel", "arbitrary"),
    )(h, gain.reshape(1, D).astype(F32), wgate, p, wp, h)


def _gdn_gate_epilogue(accs, extras, j, *, nv):
    ba = accs[0]
    a_log, dt = extras
    beta = _sigmoid(ba[:, :nv])
    x = ba[:, nv:] + dt[:, nv:]
    softplus = jnp.maximum(x, 0.0) + jnp.log(1.0 + jnp.exp(-jnp.abs(x)))
    g = -jnp.exp(a_log[:, nv:]) * softplus
    return [jnp.concatenate([beta, g], axis=1)]


def _gdn_conv_kernel(x_ref, buf_ref, w_ref, y_ref, cs_ref, xe_ref, *, tt, tc, dk, n_q_blocks, n_qk_blocks):
    cb = pl.program_id(1)
    t = pl.program_id(2)

    @pl.when(t == 0)
    def _():
        xe_ref[0:8, :] = buf_ref[0]

    @pl.when(t > 0)
    def _():
        xe_ref[0:8, :] = xe_ref[tt:tt + 8, :]

    xe_ref[8:8 + tt, :] = x_ref[0]
    w = w_ref[...]
    acc = (w[3:4] * xe_ref[8:8 + tt, :] + w[2:3] * xe_ref[7:7 + tt, :]
           + w[1:2] * xe_ref[6:6 + tt, :] + w[0:1] * xe_ref[5:5 + tt, :])
    y = _silu(acc)
    cs_ref[0] = xe_ref[tt:tt + 8, :]

    @pl.when(cb < n_qk_blocks)
    def _():
        scale = jnp.where(cb < n_q_blocks, dk ** -0.5, 1.0).astype(F32)
        for h in range(tc // dk):
            yh = y[:, h * dk:(h + 1) * dk]
            y_ref[0, :, h * dk:(h + 1) * dk] = (
                yh * lax.rsqrt(jnp.sum(yh * yh, axis=-1, keepdims=True) + EPS) * scale)

    @pl.when(cb >= n_qk_blocks)
    def _():
        y_ref[0] = y


def _gdn_conv(qkvz, buf8, conv_w, *, conv_dim, qk_dim, dk, tt=256, tc=1024):
    B, T, _ = qkvz.shape
    tt = _pick_tile(T, tt)
    assert tt >= 8 and conv_dim % tc == 0 and qk_dim % tc == 0
    kern = functools.partial(_gdn_conv_kernel, tt=tt, tc=tc, dk=dk, n_q_blocks=qk_dim // tc,
                             n_qk_blocks=2 * qk_dim // tc)
    return pl.pallas_call(
        kern,
        grid=(B, conv_dim // tc, T // tt),
        in_specs=[pl.BlockSpec((1, tt, tc), lambda b, c, t: (b, t, c)),
                  pl.BlockSpec((1, 8, tc), lambda b, c, t: (b, 0, c)),
                  pl.BlockSpec((4, tc), lambda b, c, t: (0, c))],
        out_specs=[pl.BlockSpec((1, tt, tc), lambda b, c, t: (b, t, c)),
                   pl.BlockSpec((1, 8, tc), lambda b, c, t: (b, 0, c))],
        out_shape=[jax.ShapeDtypeStruct((B, T, conv_dim), F32),
                   jax.ShapeDtypeStruct((B, 8, conv_dim), F32)],
        scratch_shapes=[pltpu.VMEM((tt + 8, tc), F32)],
        compiler_params=_cparams("parallel", "parallel", "arbitrary"),
    )(qkvz, buf8, conv_w)


def _unit_lower_inverse(a, n):
    rows = lax.broadcasted_iota(I32, (n, n), 0)
    cols = lax.broadcasted_iota(I32, (n, n), 1)
    x = jnp.where(rows == cols, 1.0, 0.0).astype(F32) - a
    p = a
    m = 2
    while m < n:
        p = _dot_hi(p, p)
        x = x + _dot_hi(x, p)
        m *= 2
    return x


def _gdn_scan_kernel(q_ref, k_ref, v_ref, z_ref, gc_ref, gr_ref, *rest, C, cps, Gv, dk, dv, rep, has_s0):
    if has_s0:
        s0_ref, gon_ref, o_ref, sN_ref, S_scr = rest
    else:
        gon_ref, o_ref, sN_ref, S_scr = rest
    t = pl.program_id(2)

    @pl.when(t == 0)
    def _():
        if has_s0:
            S_scr[...] = s0_ref[0]
        else:
            S_scr[...] = jnp.zeros_like(S_scr)

    rows = lax.broadcasted_iota(I32, (C, C), 0)
    cols = lax.broadcasted_iota(I32, (C, C), 1)
    incl = rows >= cols
    strict = rows > cols
    ltri = jnp.where(incl, 1.0, 0.0).astype(F32)
    gon = gon_ref[...]

    for c in range(cps):
        r0 = c * C
        gcm = gc_ref[0, 0, r0:r0 + C, :]
        grm = gr_ref[0, 0, :, r0:r0 + C]
        gcum_col = _dot_hi(ltri, gcm[:, Gv:])
        gcum_row = lax.dot_general(grm[Gv:, :], ltri, (((1,), (1,)), ((), ())),
                                   preferred_element_type=F32, precision=HI)
        for hh in range(Gv):
            kh = hh // rep
            qh = q_ref[0, r0:r0 + C, kh * dk:(kh + 1) * dk]
            kk = k_ref[0, r0:r0 + C, kh * dk:(kh + 1) * dk]
            vh = v_ref[0, r0:r0 + C, hh * dv:(hh + 1) * dv]
            beta = gcm[:, hh:hh + 1]
            gcol = gcum_col[:, hh:hh + 1]
            grow = gcum_row[hh:hh + 1, :]
            decay = jnp.exp(jnp.where(incl, gcol - grow, NEG))
            eg = jnp.exp(gcol)
            glast = grow[:, C - 1:C]
            kb = kk * beta
            kq = _dot_nt(jnp.concatenate([kb, qh], axis=0).astype(BF16), kk.astype(BF16))
            a = jnp.where(strict, kq[:C] * decay, 0.0)
            qk = kq[C:] * decay
            tinv = _unit_lower_inverse(a, C)
            sol = _dot_hi(tinv, jnp.concatenate([vh * beta, kb * eg], axis=1))
            u = sol[:, :dv]
            w = sol[:, dv:]
            S = S_scr[hh]
            ws = _dot(jnp.concatenate([w, qh * eg], axis=0).astype(BF16), S.astype(BF16))
            v_new = u - ws[:C]
            v_new_b = v_new.astype(BF16)
            o = ws[C:] + _dot(qk.astype(BF16), v_new_b)
            kd = (kk * jnp.exp(glast - gcol)).astype(BF16)
            S_scr[hh] = S * jnp.exp(glast) + _dot_tn(kd, v_new_b)
            zz = z_ref[0, r0:r0 + C, hh * dv:(hh + 1) * dv]
            o_ref[0, r0:r0 + C, hh * dv:(hh + 1) * dv] = (_rms(o, gon) * _silu(zz)).astype(o_ref.dtype)

    @pl.when(t == pl.num_programs(2) - 1)
    def _():
        sN_ref[0] = S_scr[...]


def _gdn_scan(y, qkvz, gates, s0, g_onorm, *, nk, nv, dk, dv, Gv=4):
    B, T, conv_dim = y.shape
    rep = nv // nk
    C = min(CHUNK, T)
    cps = 1 if C == T else max(1, LANES // C)
    TB = C * cps
    assert T % TB == 0 and Gv % rep == 0 and nv % Gv == 0
    Gk = Gv // rep
    HB = nv // Gv
    beta = gates[..., :nv].reshape(B, T, HB, Gv)
    g = gates[..., nv:].reshape(B, T, HB, Gv)
    gcol = jnp.transpose(jnp.concatenate([beta, g], axis=-1), (0, 2, 1, 3))
    grow = jnp.transpose(gcol, (0, 1, 3, 2))
    k_off = (nk * dk) // (Gk * dk)
    v_off = (2 * nk * dk) // (Gv * dv)
    z_off = conv_dim // (Gv * dv)
    in_specs = [pl.BlockSpec((1, TB, Gk * dk), lambda b, h, t: (b, t, h)),
                pl.BlockSpec((1, TB, Gk * dk), lambda b, h, t: (b, t, k_off + h)),
                pl.BlockSpec((1, TB, Gv * dv), lambda b, h, t: (b, t, v_off + h)),
                pl.BlockSpec((1, TB, Gv * dv), lambda b, h, t: (b, t, z_off + h)),
                pl.BlockSpec((1, 1, TB, 2 * Gv), lambda b, h, t: (b, h, t, 0)),
                pl.BlockSpec((1, 1, 2 * Gv, TB), lambda b, h, t: (b, h, 0, t))]
    args = [y, y, y, qkvz, gcol, grow]
    if s0 is not None:
        in_specs.append(pl.BlockSpec((1, Gv, dk, dv), lambda b, h, t: (b, h, 0, 0)))
        args.append(s0)
    in_specs.append(pl.BlockSpec((1, dv), lambda b, h, t: (0, 0)))
    args.append(g_onorm.reshape(1, dv).astype(F32))
    kern = functools.partial(_gdn_scan_kernel, C=C, cps=cps, Gv=Gv, dk=dk, dv=dv, rep=rep,
                             has_s0=s0 is not None)
    return pl.pallas_call(
        kern,
        grid=(B, HB, T // TB),
        in_specs=in_specs,
        out_specs=[pl.BlockSpec((1, TB, Gv * dv), lambda b, h, t: (b, t, h)),
                   pl.BlockSpec((1, Gv, dk, dv), lambda b, h, t: (b, h, 0, 0))],
        out_shape=[jax.ShapeDtypeStruct((B, T, nv * dv), BF16),
                   jax.ShapeDtypeStruct((B, nv, dk, dv), F32)],
        scratch_shapes=[pltpu.VMEM((Gv, dk, dv), F32)],
        compiler_params=_cparams("parallel", "parallel", "arbitrary"),
    )(*args)


def _gdn_layer(h3, conv_buf, s0, gain, w_main, w_ba, conv_w, a_log, dt_bias, g_onorm, w_out, *, nk, nv, dk, dv):
    B, T, D = h3.shape
    M = B * T
    qk_dim = nk * dk
    conv_dim = 2 * qk_dim + nv * dv
    h2 = h3.reshape(M, D)
    qkvz = _mm(h2, [w_main], _ep_plain, [F32], gain=gain)[0].reshape(B, T, -1)
    pad = jnp.zeros((1, nv), F32)
    gates = _mm(h2, [w_ba], functools.partial(_gdn_gate_epilogue, nv=nv), [F32], gain=gain,
                extras=[jnp.concatenate([pad, a_log.reshape(1, nv)], axis=1),
                        jnp.concatenate([pad, dt_bias.reshape(1, nv)], axis=1)])[0].reshape(B, T, 2 * nv)
    if conv_buf is None:
        buf8 = jnp.zeros((B, 8, conv_dim), F32)
    else:
        buf8 = jnp.pad(conv_buf, ((0, 0), (8 - conv_buf.shape[1], 0), (0, 0)))
    y, last8 = _gdn_conv(qkvz, buf8, conv_w, conv_dim=conv_dim, qk_dim=qk_dim, dk=dk)
    o, s_new = _gdn_scan(y, qkvz, gates, s0, g_onorm, nk=nk, nv=nv, dk=dk, dv=dv)
    h_new = _mm(o.reshape(M, nv * dv), [w_out], _ep_residual, [F32], extras=[h2])[0]
    return h_new.reshape(B, T, D), last8[:, 5:, :], s_new


def _t5_bucket_np(rel):
    rel = np.asarray(rel, np.int32)
    half = NUM_BUCKETS // 2
    exact = half // 2
    base = np.where(rel > 0, half, 0)
    n = np.abs(rel)
    nf = np.maximum(n, 1).astype(np.float32)
    large = exact + (np.log(nf / np.float32(exact)) / np.float32(math.log(MAX_DISTANCE / exact))
                     * np.float32(half - exact)).astype(np.int32)
    large = np.minimum(large, half - 1)
    return base + np.where(n < exact, n, large)


def _lookup_kernel(oh_ref, tab_ref, o_ref):
    o_ref[...] = _dot_hi(oh_ref[...], tab_ref[...])


def _bias_lookup(buckets, rel_bias, far_bucket=None):
    flat = np.asarray(buckets).reshape(-1)
    nb, H = rel_bias.shape
    onehot = np.zeros((flat.size, nb), np.float32)
    onehot[np.arange(flat.size), flat] = 1.0
    if far_bucket is not None:
        onehot[:, far_bucket] -= 1.0
    R_ = flat.size
    tr = _pick_tile(R_, 2048)
    out = pl.pallas_call(
        _lookup_kernel,
        grid=(R_ // tr,),
        in_specs=[pl.BlockSpec((tr, nb), lambda i: (i, 0)), pl.BlockSpec((nb, H), lambda i: (0, 0))],
        out_specs=pl.BlockSpec((tr, H), lambda i: (i, 0)),
        out_shape=jax.ShapeDtypeStruct((R_, H), F32),
        compiler_params=_cparams("parallel"),
    )(jnp.asarray(onehot), rel_bias.astype(F32))
    return out.reshape(tuple(np.asarray(buckets).shape) + (H,))


def _sortable_key(x):
    bits = pltpu.bitcast(x, I32)
    return bits ^ ((bits >> 31) & 0x7FFFFFFF)


def _topk_threshold(keys_ref, n_tiles, tile, rows, topk, idx_bits):
    sub = tile // LANES

    def count(pred):
        def body(c, acc):
            cs = pl.multiple_of(c * tile, tile)
            for s in range(sub):
                kt = keys_ref[:, pl.ds(cs + s * LANES, LANES)]
                col = cs + s * LANES + lax.broadcasted_iota(I32, (rows, LANES), 1)
                acc = acc + pred(kt, col)
            return acc
        acc = lax.fori_loop(0, n_tiles, body, jnp.zeros((rows, LANES), F32))
        return jnp.sum(acc, axis=1, keepdims=True)

    def count_ge(cand):
        return count(lambda kt, col: jnp.where(kt >= cand, 1.0, 0.0))

    kf = float(topk)
    zero = jnp.zeros((rows, LANES), I32)
    prefix = jnp.where(count_ge(zero) >= kf, zero, jnp.full((rows, LANES), INT_MIN, I32))

    def bit_body(it, prefix):
        cand = prefix | jnp.left_shift(jnp.int32(1), 30 - it)
        return jnp.where(count_ge(cand) >= kf, cand, prefix)

    thr = lax.fori_loop(0, 31, bit_body, prefix)
    n_ge = count_ge(thr)

    @pl.when(jnp.max(n_ge) > kf)
    def _():
        n_gt = count(lambda kt, col: jnp.where(kt > thr, 1.0, 0.0))
        need = kf - n_gt

        def idx_body(it, p):
            cand = p | jnp.left_shift(jnp.int32(1), idx_bits - 1 - it)
            c = count(lambda kt, col: jnp.where(kt == thr, jnp.where(col < cand, 1.0, 0.0), 0.0))
            return jnp.where(c < need, cand, p)

        p = lax.fori_loop(0, idx_bits, idx_body, zero)

        def demote(c, carry):
            cs = pl.multiple_of(c * tile, tile)
            for s in range(sub):
                sl = pl.ds(cs + s * LANES, LANES)
                kt = keys_ref[:, sl]
                col = cs + s * LANES + lax.broadcasted_iota(I32, (rows, LANES), 1)
                keys_ref[:, sl] = jnp.where(kt == thr, jnp.where(col > p, kt - 1, kt), kt)
            return carry

        lax.fori_loop(0, n_tiles, demote, 0)

    return thr


def _softmax_step(s, vt, m_ref, l_ref, acc_ref):
    m_prev = m_ref[...]
    m_next = jnp.maximum(m_prev, jnp.max(s, axis=1, keepdims=True))
    alpha = jnp.exp(m_prev - m_next)
    p = jnp.exp(s - m_next[:, :1])
    l_ref[...] = alpha * l_ref[...] + jnp.sum(p, axis=1, keepdims=True)
    acc_ref[...] = alpha * acc_ref[...] + _dot(p.astype(BF16), vt)
    m_ref[...] = m_next


def _dsa_prompt_kernel(q_ref, qi_ref, wi_ref, ki_ref, k_ref, v_ref, bias_ref, o_ref,
                       keys_ref, thr_ref, m_ref, l_ref, acc_ref,
                       *, tq, ts, topk, n_kv, rep, dh, idx_h, idx_d, chunk, idx_bits):
    i = pl.program_id(1)
    nvis = (i + 1) * tq
    n_st = (nvis + ts - 1) // ts
    row = lax.broadcasted_iota(I32, (tq, 1), 0)
    limit = i * tq + (row // chunk + 1) * chunk

    wi = wi_ref[0]
    qi = qi_ref[0]
    qis = [qi[:, h * idx_d:(h + 1) * idx_d] for h in range(idx_h)]
    wis = [wi[:, h:h + 1] for h in range(idx_h)]

    def score_body(t, carry):
        ks = pl.multiple_of(t * ts, ts)
        kt = ki_ref[0, pl.ds(ks, ts), :]
        sc = jnp.zeros((tq, ts), F32)
        for h in range(idx_h):
            sc = sc + wis[h] * jnp.maximum(_dot_nt(qis[h], kt), 0.0)
        sc = sc * (idx_d ** -0.5)
        col = ks + lax.broadcasted_iota(I32, (tq, ts), 1)
        sc = jnp.where(col < limit, sc, -jnp.inf)
        keys_ref[:, pl.ds(ks, ts)] = _sortable_key(sc)
        return carry

    lax.fori_loop(0, n_st, score_body, 0)

    thr_ref[...] = jnp.full((tq, LANES), INT_MIN, I32)

    @pl.when(nvis > topk)
    def _():
        thr_ref[...] = _topk_threshold(keys_ref, n_st, ts, tq, topk, idx_bits)

    thr = thr_ref[:, :1]
    scale = dh ** -0.5
    q = q_ref[0]

    for g in range(n_kv):
        qg = jnp.concatenate([q[:, (g * rep + r) * dh:(g * rep + r + 1) * dh] for r in range(rep)], axis=0)
        m_ref[...] = jnp.full(m_ref.shape, NEG, F32)
        l_ref[...] = jnp.zeros(l_ref.shape, F32)
        acc_ref[...] = jnp.zeros(acc_ref.shape, F32)

        def attend(ks, tk, bias, diagonal):
            kt = k_ref[0, pl.ds(ks, tk), g * dh:(g + 1) * dh]
            vt = v_ref[0, pl.ds(ks, tk), g * dh:(g + 1) * dh]
            s = _dot_nt(qg, kt) * scale
            if bias is not None:
                s = s + bias
            keep = keys_ref[:, pl.ds(ks, tk)] >= thr
            if diagonal:
                col = ks + lax.broadcasted_iota(I32, (tq, tk), 1)
                madd = jnp.where(keep, jnp.where(col < limit, 0.0, NEG), NEG)
            else:
                madd = jnp.where(keep, 0.0, NEG)
            s = s + jnp.concatenate([madd] * rep, axis=0)
            _softmax_step(s, vt, m_ref, l_ref, acc_ref)

        n_far = jnp.maximum(i - 1, 0)

        def far_body(t, carry):
            attend(pl.multiple_of(t * 2 * tq, 2 * tq), 2 * tq, None, False)
            return carry

        lax.fori_loop(0, n_far // 2, far_body, 0)

        @pl.when(n_far % 2 == 1)
        def _():
            attend(pl.multiple_of((n_far - 1) * tq, tq), tq, None, False)

        @pl.when(i >= 1)
        def _():
            attend(pl.multiple_of((i - 1) * tq, tq), tq, bias_ref[0, g], False)

        attend(pl.multiple_of(i * tq, tq), tq, bias_ref[1, g], True)

        out = acc_ref[...] / l_ref[...]
        for r in range(rep):
            hcol = (g * rep + r) * dh
            o_ref[0, :, hcol:hcol + dh] = out[r * tq:(r + 1) * tq].astype(o_ref.dtype)


def _dsa_prompt(qa, wi, ki, kb, vb, rel_bias, *, n_heads, n_kv, dh, idx_h, idx_d, tq=128):
    B, T, _ = qa.shape
    assert T % tq == 0 and tq % CHUNK == 0
    topk = min(TOPK_MAX, T // 4)
    rep = n_heads // n_kv
    qd = n_heads * dh
    idd = idx_h * idx_d
    assert qd % idd == 0
    ts = _pick_tile(T, 512)
    far_bucket = int(_t5_bucket_np(np.array([-(tq + 1)]))[0])
    assert np.all(_t5_bucket_np(-np.arange(tq + 1, 4 * T)) == far_bucket)
    dq = np.arange(tq)[:, None]
    dkk = np.arange(tq)[None, :]
    buckets = np.stack([_t5_bucket_np(dkk - tq - dq), _t5_bucket_np(dkk - dq)])
    bias = _bias_lookup(buckets, rel_bias, far_bucket)
    bias = jnp.transpose(bias.reshape(2, tq, tq, n_kv, rep), (0, 3, 4, 1, 2)).reshape(2, n_kv, rep * tq, tq)
    kern = functools.partial(_dsa_prompt_kernel, tq=tq, ts=ts, topk=topk, n_kv=n_kv, rep=rep, dh=dh,
                             idx_h=idx_h, idx_d=idx_d, chunk=CHUNK, idx_bits=max(1, (T - 1).bit_length()))
    return pl.pallas_call(
        kern,
        grid=(B, T // tq),
        in_specs=[pl.BlockSpec((1, tq, qd), lambda b, i: (b, i, 0)),
                  pl.BlockSpec((1, tq, idd), lambda b, i: (b, i, qd // idd)),
                  pl.BlockSpec((1, tq, idx_h), lambda b, i: (b, i, 0)),
                  pl.BlockSpec((1, T, idx_d), lambda b, i: (b, 0, 0)),
                  pl.BlockSpec((1, T, n_kv * dh), lambda b, i: (b, 0, 0)),
                  pl.BlockSpec((1, T, n_kv * dh), lambda b, i: (b, 0, 0)),
                  pl.BlockSpec((2, n_kv, rep * tq, tq), lambda b, i: (0, 0, 0, 0))],
        out_specs=pl.BlockSpec((1, tq, qd), lambda b, i: (b, i, 0)),
        out_shape=jax.ShapeDtypeStruct((B, T, qd), BF16),
        scratch_shapes=[pltpu.VMEM((tq, T), I32), pltpu.VMEM((tq, LANES), I32),
                        pltpu.VMEM((rep * tq, LANES), F32), pltpu.VMEM((rep * tq, LANES), F32),
                        pltpu.VMEM((rep * tq, dh), F32)],
        compiler_params=_cparams("parallel", "arbitrary"),
    )(qa, qa, wi, ki, kb, vb, bias)


def _dsa_sample_kernel(q_ref, qi_ref, wi_ref, ki_ref, k_ref, v_ref, bias_ref, o_ref, keys_ref,
                       *, tq, S, n_valid, topk, n_kv, rep, dh, idx_h, idx_d, idx_bits):
    wi = wi_ref[0]
    qi = qi_ref[0]
    kt = ki_ref[0]
    sc = jnp.zeros((tq, S), F32)
    for h in range(idx_h):
        sc = sc + wi[:, h:h + 1] * jnp.maximum(_dot_nt(qi[:, h * idx_d:(h + 1) * idx_d], kt), 0.0)
    sc = sc * (idx_d ** -0.5)
    col = lax.broadcasted_iota(I32, (tq, S), 1)
    valid = col < n_valid
    keys_ref[...] = _sortable_key(jnp.where(valid, sc, -jnp.inf))
    thr = _topk_threshold(keys_ref, S // LANES, LANES, tq, topk, idx_bits)[:, :1]
    madd = jnp.where(keys_ref[...] >= thr, jnp.where(valid, 0.0, NEG), NEG)
    madd = jnp.concatenate([madd] * rep, axis=0)
    scale = dh ** -0.5
    q = q_ref[0]
    for g in range(n_kv):
        qg = jnp.concatenate([q[:, (g * rep + r) * dh:(g * rep + r + 1) * dh] for r in range(rep)], axis=0)
        s = _dot_nt(qg, k_ref[0, :, g * dh:(g + 1) * dh]) * scale + bias_ref[g] + madd
        p = jnp.exp(s - jnp.max(s, axis=1, keepdims=True))
        out = _dot(p.astype(BF16), v_ref[0, :, g * dh:(g + 1) * dh]) / jnp.sum(p, axis=1, keepdims=True)
        for r in range(rep):
            hcol = (g * rep + r) * dh
            o_ref[0, :, hcol:hcol + dh] = out[r * tq:(r + 1) * tq].astype(o_ref.dtype)


def _dsa_sample(qa, wi, ki_all, k_all, v_all, rel_bias, *, past, n_heads, n_kv, dh, idx_h, idx_d):
    B, tq, _ = qa.shape
    S = k_all.shape[1]
    n_valid = past + tq
    topk = min(TOPK_MAX, n_valid // 4)
    rep = n_heads // n_kv
    qd = n_heads * dh
    idd = idx_h * idx_d
    rel = np.minimum(np.arange(S), n_valid - 1)[None, :] - (past + np.arange(tq))[:, None]
    bias = _bias_lookup(_t5_bucket_np(rel), rel_bias)
    bias = jnp.transpose(bias.reshape(tq, S, n_kv, rep), (2, 3, 0, 1)).reshape(n_kv, rep * tq, S)
    kern = functools.partial(_dsa_sample_kernel, tq=tq, S=S, n_valid=n_valid, topk=topk, n_kv=n_kv, rep=rep,
                             dh=dh, idx_h=idx_h, idx_d=idx_d, idx_bits=max(1, (S - 1).bit_length()))
    return pl.pallas_call(
        kern,
        grid=(B,),
        in_specs=[pl.BlockSpec((1, tq, qd), lambda b: (b, 0, 0)),
                  pl.BlockSpec((1, tq, idd), lambda b: (b, 0, qd // idd)),
                  pl.BlockSpec((1, tq, idx_h), lambda b: (b, 0, 0)),
                  pl.BlockSpec((1, S, idx_d), lambda b: (b, 0, 0)),
                  pl.BlockSpec((1, S, n_kv * dh), lambda b: (b, 0, 0)),
                  pl.BlockSpec((1, S, n_kv * dh), lambda b: (b, 0, 0)),
                  pl.BlockSpec((n_kv, rep * tq, S), lambda b: (0, 0, 0))],
        out_specs=pl.BlockSpec((1, tq, qd), lambda b: (b, 0, 0)),
        out_shape=jax.ShapeDtypeStruct((B, tq, qd), BF16),
        scratch_shapes=[pltpu.VMEM((tq, S), I32)],
        compiler_params=_cparams("parallel"),
    )(qa, qa, wi, ki_all, k_all, v_all, bias)


def _dsa_tail_epilogue(accs, extras, j, *, idx_d, idx_h):
    a = accs[0]
    ki = _rms(a[:, :idx_d], extras[0][:, :idx_d])
    return [jnp.concatenate([ki, a[:, idx_d:] * idx_h ** -0.5], axis=1)]


def _dsa_layer(h3, cache, gain, w_qa, w_kv, w_tail, g_q, g_k, g_ki, w_out, rel_bias,
               *, n_heads, n_kv, dh, idx_h, idx_d):
    B, T, D = h3.shape
    M = B * T
    qd = n_heads * dh
    kvd = n_kv * dh
    h2 = h3.reshape(M, D)
    tn = 512
    g_qa = jnp.concatenate([jnp.tile(g_q, n_heads), jnp.ones((idx_h * idx_d,), F32)]).reshape(1, -1)
    qa = _mm(h2, [w_qa], functools.partial(_ep_headnorm, dh=dh, n_norm_tiles=qd // tn), [BF16],
             gain=gain, extras=[g_qa], tn=tn)[0].reshape(B, T, -1)
    g_kv = jnp.concatenate([jnp.tile(g_k, n_kv), jnp.ones((kvd,), F32)]).reshape(1, -1)
    kv = _mm(h2, [w_kv], functools.partial(_ep_headnorm, dh=dh, n_norm_tiles=kvd // tn), [F32],
             gain=gain, extras=[g_kv], tn=tn)[0]
    g_tail = jnp.concatenate([g_ki, jnp.ones((idx_h,), F32)]).reshape(1, -1)
    tail = _mm(h2, [w_tail], functools.partial(_dsa_tail_epilogue, idx_d=idx_d, idx_h=idx_h), [F32],
               gain=gain, extras=[g_tail])[0]
    k_new = kv[:, :kvd].reshape(B, T, kvd)
    v_new = kv[:, kvd:].reshape(B, T, kvd)
    ki_new = tail[:, :idx_d].reshape(B, T, idx_d)
    wi = tail[:, idx_d:].reshape(B, T, idx_h)
    dims = dict(n_heads=n_heads, n_kv=n_kv, dh=dh, idx_h=idx_h, idx_d=idx_d)
    if cache is None:
        att = _dsa_prompt(qa, wi, ki_new.astype(BF16), k_new.astype(BF16), v_new.astype(BF16), rel_bias, **dims)
    else:
        ck, cv, cki = cache
        past = ck.shape[1]
        S = -(-(past + T) // LANES) * LANES
        padr = ((0, 0), (0, S - past - T), (0, 0))
        k_all = jnp.pad(jnp.concatenate([ck.reshape(B, past, kvd), k_new], axis=1).astype(BF16), padr)
        v_all = jnp.pad(jnp.concatenate([cv.reshape(B, past, kvd), v_new], axis=1).astype(BF16), padr)
        ki_all = jnp.pad(jnp.concatenate([cki, ki_new], axis=1).astype(BF16), padr)
        att = _dsa_sample(qa, wi, ki_all, k_all, v_all, rel_bias, past=past, **dims)
    h_new = _mm(att.reshape(M, qd), [w_out], _ep_residual, [F32], extras=[h2])[0]
    return (h_new.reshape(B, T, D), k_new.reshape(B, T, n_kv, dh), v_new.reshape(B, T, n_kv, dh), ki_new)


def _router_kernel(h_ref, g_ref, wr_ref, xn_ref, ro_ref, *, n_exp):
    x = _rms(h_ref[...], g_ref[...])
    xn_ref[...] = x.astype(BF16)
    logits = _dot_hi(x, wr_ref[...])
    lane = lax.broadcasted_iota(I32, logits.shape, 1).astype(F32)
    lg = jnp.where(lane < n_exp, logits, -jnp.inf)
    m1 = jnp.max(lg, axis=1, keepdims=True)
    i1 = jnp.min(jnp.where(lg == m1, lane, float(LANES)), axis=1, keepdims=True)
    lg2 = jnp.where(lane == i1, -jnp.inf, lg)
    m2 = jnp.max(lg2, axis=1, keepdims=True)
    i2 = jnp.min(jnp.where(lg2 == m2, lane, float(LANES)), axis=1, keepdims=True)
    e = jnp.exp(m2 - m1)
    g1 = 1.0 / (1.0 + e)
    g2 = e / (1.0 + e)
    ro_ref[...] = jnp.where(lane == 0, i1, jnp.where(lane == 1, i2, jnp.where(lane == 2, g1,
                            jnp.where(lane == 3, g2, 0.0))))


def _router(h2, gain, w_router, *, tm=512):
    M, D = h2.shape
    n_exp = w_router.shape[1]
    tm = _pick_tile(M, tm)
    wr = jnp.pad(w_router.astype(F32), ((0, 0), (0, LANES - n_exp)))
    return pl.pallas_call(
        functools.partial(_router_kernel, n_exp=n_exp),
        grid=(M // tm,),
        in_specs=[pl.BlockSpec((tm, D), lambda i: (i, 0)), pl.BlockSpec((1, D), lambda i: (0, 0)),
                  pl.BlockSpec((D, LANES), lambda i: (0, 0))],
        out_specs=[pl.BlockSpec((tm, D), lambda i: (i, 0)), pl.BlockSpec((tm, LANES), lambda i: (i, 0))],
        out_shape=[jax.ShapeDtypeStruct((M, D), BF16), jax.ShapeDtypeStruct((M, LANES), F32)],
        compiler_params=_cparams("parallel"),
    )(h2, gain.reshape(1, D).astype(F32), wr)


def _moe(h_list, gain, w_router, wg, wu, wd, *, bm=512):
    n_exp = w_router.shape[1]
    routed = [_router(h, gain, w_router) for h in h_list]
    xn = jnp.concatenate([r[0] for r in routed], axis=0)
    ro = jnp.concatenate([r[1] for r in routed], axis=0)
    N, D = xn.shape
    A = N * MOE_TOP_K
    flat_e = ro[:, :MOE_TOP_K].astype(I32).reshape(-1)
    gates = ro[:, MOE_TOP_K:2 * MOE_TOP_K]
    onehot = (flat_e[:, None] == jnp.arange(n_exp, dtype=I32)[None, :]).astype(I32)
    csum = jnp.cumsum(onehot, axis=0)
    rank = jnp.take_along_axis(csum, flat_e[:, None], axis=1)[:, 0] - 1
    counts = csum[-1]
    padded = (counts + bm - 1) // bm * bm
    pad_end = jnp.cumsum(padded)
    dest = (pad_end - padded)[flat_e] + rank
    n_blocks = (A + n_exp * (bm - 1)) // bm
    row_src = jnp.zeros((n_blocks * bm,), I32).at[dest].set(jnp.arange(A, dtype=I32) // MOE_TOP_K)
    block_e = jnp.minimum(jnp.searchsorted(pad_end, jnp.arange(n_blocks, dtype=I32) * bm, side='right'),
                          n_exp - 1).astype(I32)
    xs = jnp.take(xn, row_src, axis=0)
    yp = _ffn_experts(xs, block_e, wg, wu, wd, bm=bm)
    y = jnp.sum(jnp.take(yp, dest, axis=0).reshape(N, MOE_TOP_K, D) * gates[..., None], axis=1)
    outs = []
    off = 0
    for h in h_list:
        outs.append(h + y[off:off + h.shape[0]])
        off += h.shape[0]
    return outs


def kernel(x_prompt, x_sample, state_gdn, state_gdn_conv, cache_dsa_k, cache_dsa_v, cache_dsa_kidx, p_prompt, p_sample, g_mix, g_ffn, g_ple, w_ple_in, w_ple_gate, w_in_gdn, conv_w_gdn, a_log_gdn, dt_bias_gdn, g_onorm_gdn, w_out_gdn, w_in_dsa, g_qnorm_dsa, g_knorm_dsa, g_kidx_norm_dsa, w_out_dsa, rel_bias, w_gate_ffn, w_up_ffn, w_down_ffn, w_router, w_gate_moe, w_up_moe, w_down_moe):
    depth = g_mix.shape[0]
    Bp, Tp, D = x_prompt.shape
    Bs, Ts, _ = x_sample.shape
    nv = a_log_gdn.shape[1]
    dk, dv = state_gdn.shape[3], state_gdn.shape[4]
    conv_dim = conv_w_gdn.shape[2]
    nk = (conv_dim - nv * dv) // (2 * dk)
    gdn_dims = dict(nk=nk, nv=nv, dk=dk, dv=dv)
    n_heads = rel_bias.shape[1]
    n_kv, dh = cache_dsa_k.shape[3], cache_dsa_k.shape[4]
    idx_d = cache_dsa_kidx.shape[3]
    qd, kvd = n_heads * dh, n_kv * dh
    idx_h = (w_in_dsa.shape[2] - qd - 2 * kvd - idx_d) // (idx_d + 1)
    dsa_dims = dict(n_heads=n_heads, n_kv=n_kv, dh=dh, idx_h=idx_h, idx_d=idx_d)

    hp, hs = x_prompt, x_sample
    outs = {k: [] for k in ("sp", "cp", "ss", "cs", "kp", "vp", "kip", "ks", "vs", "kis")}
    for i in range(depth):
        j = i // 2
        if i % 2 == 0:
            w_in = w_in_gdn[j]
            n_main = conv_dim + nv * dv
            wa = (g_mix[i], w_in[:, :n_main].astype(BF16), w_in[:, n_main:].astype(BF16), conv_w_gdn[j],
                  a_log_gdn[j], dt_bias_gdn[j], g_onorm_gdn[j], w_out_gdn[j].astype(BF16))
            hp, cbp, sp = _gdn_layer(hp, None, None, *wa, **gdn_dims)
            hs, cbs, ss = _gdn_layer(hs, state_gdn_conv[j], state_gdn[j], *wa, **gdn_dims)
            outs["sp"].append(sp); outs["cp"].append(cbp); outs["ss"].append(ss); outs["cs"].append(cbs)
            wf = (g_ffn[i], w_gate_ffn[j].astype(BF16), w_up_ffn[j].astype(BF16), w_down_ffn[j].astype(BF16))
            hp = _ffn_dense(hp.reshape(Bp * Tp, D), *wf).reshape(Bp, Tp, D)
            hs = _ffn_dense(hs.reshape(Bs * Ts, D), *wf).reshape(Bs, Ts, D)
        else:
            w_in = w_in_dsa[j]
            o1 = qd + 2 * kvd
            o2 = o1 + idx_h * idx_d
            w_qa = jnp.concatenate([w_in[:, :qd], w_in[:, o1:o2]], axis=1).astype(BF16)
            wb = (g_mix[i], w_qa, w_in[:, qd:o1].astype(BF16), w_in[:, o2:].astype(BF16), g_qnorm_dsa[j],
                  g_knorm_dsa[j], g_kidx_norm_dsa[j], w_out_dsa[j].astype(BF16), rel_bias)
            hp, k, v, ki = _dsa_layer(hp, None, *wb, **dsa_dims)
            outs["kp"].append(k); outs["vp"].append(v); outs["kip"].append(ki)
            hs, k, v, ki = _dsa_layer(hs, (cache_dsa_k[j], cache_dsa_v[j], cache_dsa_kidx[j]), *wb, **dsa_dims)
            outs["ks"].append(k); outs["vs"].append(v); outs["kis"].append(ki)
            hp2, hs2 = _moe([hp.reshape(Bp * Tp, D), hs.reshape(Bs * Ts, D)], g_ffn[i], w_router[j],
                            w_gate_moe[j].astype(BF16), w_up_moe[j].astype(BF16), w_down_moe[j].astype(BF16))
            hp, hs = hp2.reshape(Bp, Tp, D), hs2.reshape(Bs, Ts, D)
        wpl = (g_ple[i], w_ple_in[i].astype(BF16), w_ple_gate[i].astype(BF16))
        hp = _ple_add(hp.reshape(Bp * Tp, D), p_prompt[i].reshape(Bp * Tp, -1), *wpl).reshape(Bp, Tp, D)
        hs = _ple_add(hs.reshape(Bs * Ts, D), p_sample[i].reshape(Bs * Ts, -1), *wpl).reshape(Bs, Ts, D)
    st = lambda name: jnp.stack(outs[name])
    return (hp, hs, st("sp"), st("cp"), st("kp"), st("vp"), st("kip"),
            st("ss"), st("cs"), st("ks"), st("vs"), st("kis"))
```

```python
import functools
import math

import numpy as np
import jax
import jax.numpy as jnp
from jax import lax
from jax.experimental import pallas as pl
from jax.experimental.pallas import tpu as pltpu

F32 = jnp.float32
BF16 = jnp.bfloat16
I32 = jnp.int32

EPS = 1e-6
CHUNK = 64
TOPK_MAX = 256
MOE_TOP_K = 2
NUM_BUCKETS = 32
MAX_DISTANCE = 128
NEG = -1e30
INT_MIN = -(2 ** 31)
LOG2E = math.log2(math.e)
HI = lax.Precision.HIGHEST

V7X_VMEM_LIMIT = 56 * 1024 * 1024
LANES = 128


def _cparams(*sem):
    return pltpu.CompilerParams(dimension_semantics=sem, vmem_limit_bytes=V7X_VMEM_LIMIT)


def _sigmoid(x):
    return 1.0 / (1.0 + jnp.exp(-x))


def _silu(x):
    return x * _sigmoid(x)


def _rms(x, g):
    return x * lax.rsqrt(jnp.mean(x * x, axis=-1, keepdims=True) + EPS) * g


def _dot(a, b):
    return jnp.dot(a, b, preferred_element_type=F32)


def _dot_nt(a, b):
    return lax.dot_general(a, b, (((1,), (1,)), ((), ())), preferred_element_type=F32)


def _dot_tn(a, b):
    return lax.dot_general(a, b, (((0,), (0,)), ((), ())), preferred_element_type=F32)


def _dot_hi(a, b):
    return jnp.dot(a, b, preferred_element_type=F32, precision=HI)


def _pick_tile(n, pref):
    t = min(n, pref)
    while n % t:
        t //= 2
    return t


def _mm_kernel(*refs, has_gain, n_w, n_extra, n_out, epilogue):
    x_ref = refs[0]
    pos = 1
    g_ref = None
    if has_gain:
        g_ref = refs[pos]
        pos += 1
    w_refs = refs[pos:pos + n_w]
    pos += n_w
    e_refs = refs[pos:pos + n_extra]
    pos += n_extra
    o_refs = refs[pos:pos + n_out]
    pos += n_out
    xs_ref = refs[pos]

    @pl.when(pl.program_id(1) == 0)
    def _():
        x = x_ref[...].astype(F32)
        if has_gain:
            x = _rms(x, g_ref[...])
        xs_ref[...] = x.astype(BF16)

    xs = xs_ref[...]
    accs = [_dot(xs, w[...]) for w in w_refs]
    outs = epilogue(accs, [e[...] for e in e_refs], pl.program_id(1))
    for o_ref, o in zip(o_refs, outs):
        o_ref[...] = o.astype(o_ref.dtype)


def _mm(x, ws, epilogue, out_dtypes, *, gain=None, extras=(), tm=1024, tn=512, name=None):
    M, K = x.shape
    N = ws[0].shape[1]
    tm = _pick_tile(M, tm)
    tn = _pick_tile(N, tn)
    in_specs = [pl.BlockSpec((tm, K), lambda i, j: (i, 0))]
    args = [x]
    if gain is not None:
        in_specs.append(pl.BlockSpec((1, K), lambda i, j: (0, 0)))
        args.append(gain.reshape(1, K).astype(F32))
    for w in ws:
        in_specs.append(pl.BlockSpec((K, tn), lambda i, j: (0, j)))
        args.append(w)
    for e in extras:
        if e.shape[0] == 1:
            in_specs.append(pl.BlockSpec((1, tn), lambda i, j: (0, j)))
        else:
            in_specs.append(pl.BlockSpec((tm, tn), lambda i, j: (i, j)))
        args.append(e)
    out_shape = [jax.ShapeDtypeStruct((M, N), d) for d in out_dtypes]
    out_specs = [pl.BlockSpec((tm, tn), lambda i, j: (i, j)) for _ in out_dtypes]
    kern = functools.partial(_mm_kernel, has_gain=gain is not None, n_w=len(ws), n_extra=len(extras),
                             n_out=len(out_dtypes), epilogue=epilogue)
    return pl.pallas_call(
        kern,
        grid=(M // tm, N // tn),
        in_specs=in_specs,
        out_specs=out_specs,
        out_shape=out_shape,
        scratch_shapes=[pltpu.VMEM((tm, K), BF16)],
        compiler_params=_cparams("parallel", "arbitrary"),
        name=name,
    )(*args)


def _ep_plain(accs, extras, j):
    return [accs[0]]


def _ep_residual(accs, extras, j):
    return [extras[0] + accs[0]]


def _ep_headnorm(accs, extras, j, *, dh, n_norm_tiles, post_scale=1.0):
    a = accs[0]
    g = extras[0]
    cols = []
    for h in range(a.shape[1] // dh):
        sl = slice(h * dh, (h + 1) * dh)
        cols.append(_rms(a[:, sl], g[:, sl]) * post_scale)
    normed = jnp.concatenate(cols, axis=1) if len(cols) > 1 else cols[0]
    return [jnp.where(j < n_norm_tiles, normed, a)]


def _ffn_kernel(*refs, has_gain, has_be):
    if has_be:
        used = pl.program_id(0) < refs[1][0]
        pl.when(used)(lambda: _ffn_body(refs[2:], has_gain))
        o_ref = refs[6]

        @pl.when(jnp.logical_not(used))
        def _():
            o_ref[...] = jnp.zeros_like(o_ref)
    else:
        _ffn_body(refs, has_gain)


def _ffn_body(refs, has_gain):
    pos = 0
    x_ref = refs[pos]
    pos += 1
    g_ref = None
    if has_gain:
        g_ref = refs[pos]
        pos += 1
    wg_ref, wu_ref, wd_ref, o_ref, xs_ref, acc_ref = refs[pos:pos + 6]
    j = pl.program_id(1)

    @pl.when(j == 0)
    def _():
        if has_gain:
            x = x_ref[...].astype(F32)
            xs_ref[...] = _rms(x, g_ref[...]).astype(BF16)
            acc_ref[...] = x
        else:
            xs_ref[...] = x_ref[...].astype(BF16)
            acc_ref[...] = jnp.zeros_like(acc_ref)

    xs = xs_ref[...]
    wg = wg_ref[...].reshape(wg_ref.shape[-2:])
    wu = wu_ref[...].reshape(wu_ref.shape[-2:])
    wd = wd_ref[...].reshape(wd_ref.shape[-2:])
    a = (_silu(_dot(xs, wg)) * _dot(xs, wu)).astype(BF16)
    acc_ref[...] += _dot(a, wd)

    @pl.when(j == pl.num_programs(1) - 1)
    def _():
        o_ref[...] = acc_ref[...].astype(o_ref.dtype)


def _ffn_dense(h, gain, wg, wu, wd, *, tm=512, tf=512):
    M, D = h.shape
    Fdim = wg.shape[1]
    tm = _pick_tile(M, tm)
    tf = _pick_tile(Fdim, tf)
    return pl.pallas_call(
        functools.partial(_ffn_kernel, has_gain=True, has_be=False),
        grid=(M // tm, Fdim // tf),
        in_specs=[pl.BlockSpec((tm, D), lambda i, j: (i, 0)),
                  pl.BlockSpec((1, D), lambda i, j: (0, 0)),
                  pl.BlockSpec((D, tf), lambda i, j: (0, j)),
                  pl.BlockSpec((D, tf), lambda i, j: (0, j)),
                  pl.BlockSpec((tf, D), lambda i, j: (j, 0))],
        out_specs=pl.BlockSpec((tm, D), lambda i, j: (i, 0)),
        out_shape=jax.ShapeDtypeStruct((M, D), F32),
        scratch_shapes=[pltpu.VMEM((tm, D), BF16), pltpu.VMEM((tm, D), F32)],
        compiler_params=_cparams("parallel", "arbitrary"),
        name="ffn_dense",
    )(h, gain.reshape(1, D).astype(F32), wg, wu, wd)


def _ffn_experts(xs, block_e, n_used, wg, wu, wd, *, bm, tf=512):
    P, D = xs.shape
    Fdim = wg.shape[2]
    tf = _pick_tile(Fdim, tf)
    nf = Fdim // tf

    def row(i, nu):
        return jnp.minimum(i, nu[0] - 1)

    def col(i, j, nu):
        return jnp.where(i < nu[0], j, nf - 1)

    grid_spec = pltpu.PrefetchScalarGridSpec(
        num_scalar_prefetch=2,
        grid=(P // bm, nf),
        in_specs=[pl.BlockSpec((bm, D), lambda i, j, be, nu: (row(i, nu), 0)),
                  pl.BlockSpec((1, D, tf), lambda i, j, be, nu: (be[row(i, nu)], 0, col(i, j, nu))),
                  pl.BlockSpec((1, D, tf), lambda i, j, be, nu: (be[row(i, nu)], 0, col(i, j, nu))),
                  pl.BlockSpec((1, tf, D), lambda i, j, be, nu: (be[row(i, nu)], col(i, j, nu), 0))],
        out_specs=pl.BlockSpec((bm, D), lambda i, j, be, nu: (i, 0)),
        scratch_shapes=[pltpu.VMEM((bm, D), BF16), pltpu.VMEM((bm, D), F32)],
    )
    return pl.pallas_call(
        functools.partial(_ffn_kernel, has_gain=False, has_be=True),
        grid_spec=grid_spec,
        out_shape=jax.ShapeDtypeStruct((P, D), F32),
        compiler_params=_cparams("parallel", "arbitrary"),
        name="moe_experts",
    )(block_e, n_used, xs, wg, wu, wd)


def _ple_kernel(h_ref, g_ref, wgate_ref, p_ref, wp_ref, ht_ref, o_ref, xs_ref):
    @pl.when(pl.program_id(1) == 0)
    def _():
        xs_ref[...] = _rms(h_ref[...], g_ref[...]).astype(BF16)

    gate = _sigmoid(_dot(xs_ref[...], wgate_ref[...]))
    o_ref[...] = ht_ref[...] + gate * _dot(p_ref[...].astype(BF16), wp_ref[...])


def _ple_add(h, p, gain, wp, wgate, *, tm=1024, tn=512):
    M, D = h.shape
    Pd = p.shape[1]
    tm = _pick_tile(M, tm)
    tn = _pick_tile(D, tn)
    return pl.pallas_call(
        _ple_kernel,
        grid=(M // tm, D // tn),
        in_specs=[pl.BlockSpec((tm, D), lambda i, j: (i, 0)),
                  pl.BlockSpec((1, D), lambda i, j: (0, 0)),
                  pl.BlockSpec((D, tn), lambda i, j: (0, j)),
                  pl.BlockSpec((tm, Pd), lambda i, j: (i, 0)),
                  pl.BlockSpec((Pd, tn), lambda i, j: (0, j)),
                  pl.BlockSpec((tm, tn), lambda i, j: (i, j))],
        out_specs=pl.BlockSpec((tm, tn), lambda i, j: (i, j)),
        out_shape=jax.ShapeDtypeStruct((M, D), F32),
        scratch_shapes=[pltpu.VMEM((tm, D), BF16)],
        compiler_params=_cparams("parallel", "arbitrary"),
        name="ple_add",
    )(h, gain.reshape(1, D).astype(F32), wgate, p, wp, h)


def _gdn_gate_epilogue(accs, extras, j, *, nv):
    ba = accs[0]
    a_log, dt = extras
    beta = _sigmoid(ba[:, :nv])
    x = ba[:, nv:] + dt[:, nv:]
    softplus = jnp.maximum(x, 0.0) + jnp.log(1.0 + jnp.exp(-jnp.abs(x)))
    g = -jnp.exp(a_log[:, nv:]) * softplus
    return [jnp.concatenate([beta, g], axis=1)]


def _gdn_conv_kernel(x_ref, buf_ref, w_ref, y_ref, cs_ref, xe_ref, *, tt, tc, dk, n_q_blocks, n_qk_blocks):
    cb = pl.program_id(1)
    t = pl.program_id(2)

    @pl.when(t == 0)
    def _():
        xe_ref[0:8, :] = buf_ref[0]

    @pl.when(t > 0)
    def _():
        xe_ref[0:8, :] = xe_ref[tt:tt + 8, :]

    xe_ref[8:8 + tt, :] = x_ref[0].astype(F32)
    w = w_ref[...]
    acc = (w[3:4] * xe_ref[8:8 + tt, :] + w[2:3] * xe_ref[7:7 + tt, :]
           + w[1:2] * xe_ref[6:6 + tt, :] + w[0:1] * xe_ref[5:5 + tt, :])
    y = _silu(acc)
    cs_ref[0] = xe_ref[tt:tt + 8, :]

    @pl.when(cb < n_qk_blocks)
    def _():
        scale = jnp.where(cb < n_q_blocks, dk ** -0.5, 1.0).astype(F32)
        for h in range(tc // dk):
            yh = y[:, h * dk:(h + 1) * dk]
            y_ref[0, :, h * dk:(h + 1) * dk] = (
                yh * lax.rsqrt(jnp.sum(yh * yh, axis=-1, keepdims=True) + EPS) * scale).astype(y_ref.dtype)

    @pl.when(cb >= n_qk_blocks)
    def _():
        y_ref[0] = y.astype(y_ref.dtype)


def _gdn_conv(qkvz, buf8, conv_w, *, conv_dim, qk_dim, dk, tt=256, tc=1024):
    B, T, _ = qkvz.shape
    tt = _pick_tile(T, tt)
    assert tt >= 8 and conv_dim % tc == 0 and qk_dim % tc == 0
    kern = functools.partial(_gdn_conv_kernel, tt=tt, tc=tc, dk=dk, n_q_blocks=qk_dim // tc,
                             n_qk_blocks=2 * qk_dim // tc)
    return pl.pallas_call(
        kern,
        grid=(B, conv_dim // tc, T // tt),
        in_specs=[pl.BlockSpec((1, tt, tc), lambda b, c, t: (b, t, c)),
                  pl.BlockSpec((1, 8, tc), lambda b, c, t: (b, 0, c)),
                  pl.BlockSpec((4, tc), lambda b, c, t: (0, c))],
        out_specs=[pl.BlockSpec((1, tt, tc), lambda b, c, t: (b, t, c)),
                   pl.BlockSpec((1, 8, tc), lambda b, c, t: (b, 0, c))],
        out_shape=[jax.ShapeDtypeStruct((B, T, conv_dim), BF16),
                   jax.ShapeDtypeStruct((B, 8, conv_dim), F32)],
        scratch_shapes=[pltpu.VMEM((tt + 8, tc), F32)],
        compiler_params=_cparams("parallel", "parallel", "arbitrary"),
        name="gdn_conv",
    )(qkvz, buf8, conv_w)


def _split_bf16(a):
    hi = a.astype(BF16)
    return hi, (a - hi.astype(F32)).astype(BF16)


def _dot_x3(a, b):
    ah, al = _split_bf16(a)
    bh, bl = _split_bf16(b)
    return _dot(ah, bh) + _dot(al, bh) + _dot(ah, bl)


def _gdn_scan_kernel(q_ref, k_ref, v_ref, z_ref, gc_ref, gr_ref, *rest, C, cps, Gv, dk, dv, rep, has_s0):
    if has_s0:
        s0_ref, gon_ref, o_ref, sN_ref, S_scr = rest
    else:
        gon_ref, o_ref, sN_ref, S_scr = rest
    t = pl.program_id(2)

    @pl.when(t == 0)
    def _():
        if has_s0:
            S_scr[...] = s0_ref[0]
        else:
            S_scr[...] = jnp.zeros_like(S_scr)

    rows = lax.broadcasted_iota(I32, (C, C), 0)
    cols = lax.broadcasted_iota(I32, (C, C), 1)
    incl = rows >= cols
    strict = rows > cols
    ltri = jnp.where(incl, 1.0, 0.0).astype(F32)
    eye = jnp.where(rows == cols, 1.0, 0.0).astype(F32)
    gon = gon_ref[...]

    chains = [(c, hh) for c in range(cps) for hh in range(Gv)]
    gcm, gcum_col, gcum_row = [], [], []
    for c in range(cps):
        r0 = c * C
        m = gc_ref[0, 0, r0:r0 + C, :]
        gcm.append(m)
        gcum_col.append(_dot_hi(ltri, m[:, Gv:]))
        gcum_row.append(lax.dot_general(gr_ref[0, 0, Gv:, r0:r0 + C], ltri, (((1,), (1,)), ((), ())),
                                        preferred_element_type=F32, precision=HI))

    st = []
    for (c, hh) in chains:
        r0 = c * C
        kh = hh // rep
        qh = q_ref[0, r0:r0 + C, kh * dk:(kh + 1) * dk].astype(F32)
        kk = k_ref[0, r0:r0 + C, kh * dk:(kh + 1) * dk].astype(F32)
        vh = v_ref[0, r0:r0 + C, hh * dv:(hh + 1) * dv].astype(F32)
        beta = gcm[c][:, hh:hh + 1]
        gcol = gcum_col[c][:, hh:hh + 1]
        grow = gcum_row[c][hh:hh + 1, :]
        glast = grow[:, C - 1:C]
        decay = jnp.exp(jnp.where(incl, gcol - grow, NEG))
        eg = jnp.exp(gcol)
        kb = kk * beta
        kq = _dot_nt(jnp.concatenate([kb, qh], axis=0).astype(BF16), kk.astype(BF16))
        p = jnp.where(strict, -(kq[:C] * decay), 0.0)
        st.append(dict(
            p=p, y=eye + p, qk=(kq[C:] * decay).astype(BF16),
            rhs=jnp.concatenate([vh * beta, kb * eg], axis=1),
            qe=qh * eg, kdT=jnp.transpose(kk * jnp.exp(glast - gcol)).astype(BF16),
            egl=jnp.exp(glast)))

    for s in st:
        s["p"] = _dot_x3(s["p"], s["p"])
    m = 4
    while m < C:
        for s in st:
            r = _dot_x3(jnp.concatenate([s["p"], s["y"]], axis=0), s["p"])
            s["p"] = r[:C]
            s["y"] = s["y"] + r[C:]
        m *= 2
    for s in st:
        s["y"] = s["y"] + _dot_x3(s["y"], s["p"])
    for s in st:
        sol = _dot_x3(s["y"], s["rhs"])
        s["u"] = sol[:, :dv]
        s["wq"] = jnp.concatenate([sol[:, dv:], s["qe"]], axis=0).astype(BF16)
        s["lhs2"] = jnp.concatenate([s["qk"], s["kdT"]], axis=0)

    S = [S_scr[hh] for hh in range(Gv)]
    for c in range(cps):
        r0 = c * C
        cur = [st[c * Gv + hh] for hh in range(Gv)]
        ws = [_dot(s["wq"], S[hh].astype(BF16)) for hh, s in enumerate(cur)]
        v_new = [(s["u"] - ws[hh][:C]).astype(BF16) for hh, s in enumerate(cur)]
        r2 = [_dot(s["lhs2"], v_new[hh]) for hh, s in enumerate(cur)]
        for hh, s in enumerate(cur):
            o = ws[hh][C:] + r2[hh][:C]
            S[hh] = S[hh] * s["egl"] + r2[hh][C:]
            zz = z_ref[0, r0:r0 + C, hh * dv:(hh + 1) * dv].astype(F32)
            o_ref[0, r0:r0 + C, hh * dv:(hh + 1) * dv] = (_rms(o, gon) * _silu(zz)).astype(o_ref.dtype)
    for hh in range(Gv):
        S_scr[hh] = S[hh]

    @pl.when(t == pl.num_programs(2) - 1)
    def _():
        sN_ref[0] = S_scr[...]


def _gdn_scan(y, qkvz, gates, s0, g_onorm, *, nk, nv, dk, dv, Gv=8):
    B, T, conv_dim = y.shape
    rep = nv // nk
    C = min(CHUNK, T)
    cps = 1 if C == T else max(1, LANES // C)
    TB = C * cps
    assert T % TB == 0 and Gv % rep == 0 and nv % Gv == 0
    Gk = Gv // rep
    HB = nv // Gv
    beta = gates[..., :nv].reshape(B, T, HB, Gv)
    g = gates[..., nv:].reshape(B, T, HB, Gv)
    gcol = jnp.transpose(jnp.concatenate([beta, g], axis=-1), (0, 2, 1, 3))
    grow = jnp.transpose(gcol, (0, 1, 3, 2))
    k_off = (nk * dk) // (Gk * dk)
    v_off = (2 * nk * dk) // (Gv * dv)
    z_off = conv_dim // (Gv * dv)
    in_specs = [pl.BlockSpec((1, TB, Gk * dk), lambda b, h, t: (b, t, h)),
                pl.BlockSpec((1, TB, Gk * dk), lambda b, h, t: (b, t, k_off + h)),
                pl.BlockSpec((1, TB, Gv * dv), lambda b, h, t: (b, t, v_off + h)),
                pl.BlockSpec((1, TB, Gv * dv), lambda b, h, t: (b, t, z_off + h)),
                pl.BlockSpec((1, 1, TB, 2 * Gv), lambda b, h, t: (b, h, t, 0)),
                pl.BlockSpec((1, 1, 2 * Gv, TB), lambda b, h, t: (b, h, 0, t))]
    args = [y, y, y, qkvz, gcol, grow]
    if s0 is not None:
        in_specs.append(pl.BlockSpec((1, Gv, dk, dv), lambda b, h, t: (b, h, 0, 0)))
        args.append(s0)
    in_specs.append(pl.BlockSpec((1, dv), lambda b, h, t: (0, 0)))
    args.append(g_onorm.reshape(1, dv).astype(F32))
    kern = functools.partial(_gdn_scan_kernel, C=C, cps=cps, Gv=Gv, dk=dk, dv=dv, rep=rep,
                             has_s0=s0 is not None)
    return pl.pallas_call(
        kern,
        grid=(B, HB, T // TB),
        in_specs=in_specs,
        out_specs=[pl.BlockSpec((1, TB, Gv * dv), lambda b, h, t: (b, t, h)),
                   pl.BlockSpec((1, Gv, dk, dv), lambda b, h, t: (b, h, 0, 0))],
        out_shape=[jax.ShapeDtypeStruct((B, T, nv * dv), BF16),
                   jax.ShapeDtypeStruct((B, nv, dk, dv), F32)],
        scratch_shapes=[pltpu.VMEM((Gv, dk, dv), F32)],
        compiler_params=_cparams("parallel", "parallel", "arbitrary"),
        name="gdn_scan",
    )(*args)


def _gdn_layer(h3, conv_buf, s0, gain, w_main, w_ba, conv_w, a_log, dt_bias, g_onorm, w_out, *, nk, nv, dk, dv):
    B, T, D = h3.shape
    M = B * T
    qk_dim = nk * dk
    conv_dim = 2 * qk_dim + nv * dv
    h2 = h3.reshape(M, D)
    qkvz = _mm(h2, [w_main], _ep_plain, [BF16], gain=gain, name="gdn_in_proj")[0].reshape(B, T, -1)
    pad = jnp.zeros((1, nv), F32)
    gates = _mm(h2, [w_ba], functools.partial(_gdn_gate_epilogue, nv=nv), [F32], gain=gain,
                extras=[jnp.concatenate([pad, a_log.reshape(1, nv)], axis=1),
                        jnp.concatenate([pad, dt_bias.reshape(1, nv)], axis=1)],
                name="gdn_gates")[0].reshape(B, T, 2 * nv)
    if conv_buf is None:
        buf8 = jnp.zeros((B, 8, conv_dim), F32)
    else:
        buf8 = jnp.pad(conv_buf, ((0, 0), (8 - conv_buf.shape[1], 0), (0, 0)))
    y, last8 = _gdn_conv(qkvz, buf8, conv_w, conv_dim=conv_dim, qk_dim=qk_dim, dk=dk)
    o, s_new = _gdn_scan(y, qkvz, gates, s0, g_onorm, nk=nk, nv=nv, dk=dk, dv=dv)
    h_new = _mm(o.reshape(M, nv * dv), [w_out], _ep_residual, [F32], extras=[h2], name="gdn_out_proj")[0]
    return h_new.reshape(B, T, D), last8[:, 5:, :], s_new


def _t5_bucket_np(rel):
    rel = np.asarray(rel, np.int32)
    half = NUM_BUCKETS // 2
    exact = half // 2
    base = np.where(rel > 0, half, 0)
    n = np.abs(rel)
    nf = np.maximum(n, 1).astype(np.float32)
    large = exact + (np.log(nf / np.float32(exact)) / np.float32(math.log(MAX_DISTANCE / exact))
                     * np.float32(half - exact)).astype(np.int32)
    large = np.minimum(large, half - 1)
    return base + np.where(n < exact, n, large)


def _lookup_kernel(oh_ref, tab_ref, o_ref):
    o_ref[...] = _dot_hi(oh_ref[...], tab_ref[...])


def _bias_lookup(buckets, rel_bias, far_bucket=None):
    flat = np.asarray(buckets).reshape(-1)
    nb, H = rel_bias.shape
    onehot = np.zeros((flat.size, nb), np.float32)
    onehot[np.arange(flat.size), flat] = 1.0
    if far_bucket is not None:
        onehot[:, far_bucket] -= 1.0
    R_ = flat.size
    tr = _pick_tile(R_, 2048)
    out = pl.pallas_call(
        _lookup_kernel,
        grid=(R_ // tr,),
        in_specs=[pl.BlockSpec((tr, nb), lambda i: (i, 0)), pl.BlockSpec((nb, H), lambda i: (0, 0))],
        out_specs=pl.BlockSpec((tr, H), lambda i: (i, 0)),
        out_shape=jax.ShapeDtypeStruct((R_, H), F32),
        compiler_params=_cparams("parallel"),
    )(jnp.asarray(onehot), rel_bias.astype(F32))
    return out.reshape(tuple(np.asarray(buckets).shape) + (H,))


def _sortable_key(x):
    bits = pltpu.bitcast(x, I32)
    return bits ^ ((bits >> 31) & 0x7FFFFFFF)


def _topk_threshold(keys_ref, n_tiles, tile, rows, topk, idx_bits):
    sub = tile // LANES

    def count(pred):
        def body(c, acc):
            cs = pl.multiple_of(c * tile, tile)
            for s in range(sub):
                kt = keys_ref[:, pl.ds(cs + s * LANES, LANES)]
                col = cs + s * LANES + lax.broadcasted_iota(I32, (rows, LANES), 1)
                acc = acc + pred(kt, col)
            return acc
        acc = lax.fori_loop(0, n_tiles, body, jnp.zeros((rows, LANES), F32))
        return jnp.sum(acc, axis=1, keepdims=True)

    def count_ge(cand):
        return count(lambda kt, col: jnp.where(kt >= cand, 1.0, 0.0))

    kf = float(topk)
    zero = jnp.zeros((rows, LANES), I32)
    prefix = jnp.where(count_ge(zero) >= kf, zero, jnp.full((rows, LANES), INT_MIN, I32))

    def bit_body(it, prefix):
        cand = prefix | jnp.left_shift(jnp.int32(1), 30 - it)
        return jnp.where(count_ge(cand) >= kf, cand, prefix)

    thr = lax.fori_loop(0, 31, bit_body, prefix)
    n_ge = count_ge(thr)

    @pl.when(jnp.max(n_ge) > kf)
    def _():
        n_gt = count(lambda kt, col: jnp.where(kt > thr, 1.0, 0.0))
        need = kf - n_gt

        def idx_body(it, p):
            cand = p | jnp.left_shift(jnp.int32(1), idx_bits - 1 - it)
            c = count(lambda kt, col: jnp.where(kt == thr, jnp.where(col < cand, 1.0, 0.0), 0.0))
            return jnp.where(c < need, cand, p)

        p = lax.fori_loop(0, idx_bits, idx_body, zero)

        def demote(c, carry):
            cs = pl.multiple_of(c * tile, tile)
            for s in range(sub):
                sl = pl.ds(cs + s * LANES, LANES)
                kt = keys_ref[:, sl]
                col = cs + s * LANES + lax.broadcasted_iota(I32, (rows, LANES), 1)
                keys_ref[:, sl] = jnp.where(kt == thr, jnp.where(col > p, kt - 1, kt), kt)
            return carry

        lax.fori_loop(0, n_tiles, demote, 0)

    return thr


def _dsa_prompt_kernel(q_ref, qi_ref, wi_ref, ki_ref, k_ref, v_ref, bias_ref, o_ref,
                       keys_ref, thr_ref, m_ref, acc_ref,
                       *, tq, ts, tf, topk, n_kv, rep, dh, idx_h, idx_d, chunk, idx_bits):
    i = pl.program_id(1)
    nvis = (i + 1) * tq
    n_st = (nvis + ts - 1) // ts
    row = lax.broadcasted_iota(I32, (tq, 1), 0)
    limit = i * tq + (row // chunk + 1) * chunk

    wi = wi_ref[0]
    qi = qi_ref[0]
    qis = [qi[:, h * idx_d:(h + 1) * idx_d] for h in range(idx_h)]
    wis = [wi[:, h:h + 1] for h in range(idx_h)]

    def score_body(t, carry):
        ks = pl.multiple_of(t * ts, ts)
        kt = ki_ref[0, pl.ds(ks, ts), :]
        sc = jnp.zeros((tq, ts), F32)
        for h in range(idx_h):
            sc = sc + wis[h] * jnp.maximum(_dot_nt(qis[h], kt), 0.0)
        sc = sc * (idx_d ** -0.5)
        col = ks + lax.broadcasted_iota(I32, (tq, ts), 1)
        sc = jnp.where(col < limit, sc, -jnp.inf)
        keys_ref[:, pl.ds(ks, ts)] = _sortable_key(sc)
        return carry

    lax.fori_loop(0, n_st, score_body, 0)

    thr_ref[...] = jnp.full((tq, LANES), INT_MIN, I32)

    @pl.when(nvis > topk)
    def _():
        thr_ref[...] = _topk_threshold(keys_ref, n_st, ts, tq, topk, idx_bits)

    thr = thr_ref[:, :1]
    q = q_ref[0]

    qg = [jnp.concatenate([q[:, (g * rep + r) * dh:(g * rep + r + 1) * dh] for r in range(rep)], axis=0)
          for g in range(n_kv)]
    m_ref[...] = jnp.full(m_ref.shape, NEG, F32)
    acc_ref[...] = jnp.zeros(acc_ref.shape, F32)

    def attend(ks, tk, bias_of, col_lo, col_hi):
        col = ks + lax.broadcasted_iota(I32, (tq, tk), 1)
        madd = jnp.where(keys_ref[:, pl.ds(ks, tk)] >= thr, 0.0, NEG)
        if col_lo is not None:
            madd = jnp.where(col >= col_lo, madd, NEG)
        if col_hi is not None:
            madd = jnp.where(col < col_hi, madd, NEG)
        madd = jnp.concatenate([madd] * rep, axis=0)
        ones = jnp.ones((tk, dh), BF16)
        s = []
        for g in range(n_kv):
            sg = _dot_nt(qg[g], k_ref[0, pl.ds(ks, tk), g * dh:(g + 1) * dh]) + madd
            s.append(sg if bias_of is None else sg + bias_of(g))
        for g in range(n_kv):
            m_prev = m_ref[g]
            m_next = jnp.maximum(m_prev, jnp.max(s[g], axis=1, keepdims=True))
            alpha = jnp.exp2(m_prev - m_next)
            p = jnp.exp2(s[g] - jnp.concatenate([m_next] * (tk // LANES), axis=1)).astype(BF16)
            vt = jnp.concatenate([v_ref[0, pl.ds(ks, tk), g * dh:(g + 1) * dh], ones], axis=1)
            acc_ref[g] = jnp.concatenate([alpha] * (2 * dh // LANES), axis=1) * acc_ref[g] + _dot(p, vt)
            m_ref[g] = m_next

    far_end = jnp.maximum(i - 1, 0) * tq

    def far_body(t, carry):
        attend(pl.multiple_of(t * tf, tf), tf, None, None, far_end)
        return carry

    lax.fori_loop(0, (far_end + tf - 1) // tf, far_body, 0)

    @pl.when(i == 0)
    def _():
        attend(0, tq, lambda g: bias_ref[g, :, tq:], None, limit)

    @pl.when(i > 0)
    def _():
        attend(pl.multiple_of((i - 1) * tq, tq), 2 * tq, lambda g: bias_ref[g], None, limit)

    for g in range(n_kv):
        acc = acc_ref[g]
        out = acc[:, :dh] / acc[:, dh:]
        for r in range(rep):
            hcol = (g * rep + r) * dh
            o_ref[0, :, hcol:hcol + dh] = out[r * tq:(r + 1) * tq].astype(o_ref.dtype)


def _dsa_prompt(qa, wi, ki, kb, vb, rel_bias, *, n_heads, n_kv, dh, idx_h, idx_d, tq=128):
    B, T, _ = qa.shape
    assert T % tq == 0 and tq % CHUNK == 0
    topk = min(TOPK_MAX, T // 4)
    rep = n_heads // n_kv
    qd = n_heads * dh
    idd = idx_h * idx_d
    assert qd % idd == 0
    ts = _pick_tile(T, 512)
    far_bucket = int(_t5_bucket_np(np.array([-(tq + 1)]))[0])
    assert np.all(_t5_bucket_np(-np.arange(tq + 1, 4 * T)) == far_bucket)
    dq = np.arange(tq)[:, None]
    buckets = _t5_bucket_np(np.arange(2 * tq)[None, :] - tq - dq)
    bias = _bias_lookup(buckets, rel_bias, far_bucket) * LOG2E
    bias = jnp.transpose(bias.reshape(tq, 2 * tq, n_kv, rep), (2, 3, 0, 1)).reshape(n_kv, rep * tq, 2 * tq)
    kern = functools.partial(_dsa_prompt_kernel, tq=tq, ts=ts, tf=ts, topk=topk, n_kv=n_kv, rep=rep, dh=dh,
                             idx_h=idx_h, idx_d=idx_d, chunk=CHUNK, idx_bits=max(1, (T - 1).bit_length()))
    return pl.pallas_call(
        kern,
        grid=(B, T // tq),
        in_specs=[pl.BlockSpec((1, tq, qd), lambda b, i: (b, i, 0)),
                  pl.BlockSpec((1, tq, idd), lambda b, i: (b, i, qd // idd)),
                  pl.BlockSpec((1, tq, idx_h), lambda b, i: (b, i, 0)),
                  pl.BlockSpec((1, T, idx_d), lambda b, i: (b, 0, 0)),
                  pl.BlockSpec((1, T, n_kv * dh), lambda b, i: (b, 0, 0)),
                  pl.BlockSpec((1, T, n_kv * dh), lambda b, i: (b, 0, 0)),
                  pl.BlockSpec((n_kv, rep * tq, 2 * tq), lambda b, i: (0, 0, 0))],
        out_specs=pl.BlockSpec((1, tq, qd), lambda b, i: (b, i, 0)),
        out_shape=jax.ShapeDtypeStruct((B, T, qd), BF16),
        scratch_shapes=[pltpu.VMEM((tq, T), I32), pltpu.VMEM((tq, LANES), I32),
                        pltpu.VMEM((n_kv, rep * tq, LANES), F32), pltpu.VMEM((n_kv, rep * tq, 2 * dh), F32)],
        compiler_params=_cparams("parallel", "arbitrary"),
        name="dsa_prompt",
    )(qa, qa, wi, ki, kb, vb, bias)


def _dsa_sample_kernel(q_ref, qi_ref, wi_ref, ki_ref, k_ref, v_ref, bias_ref, o_ref, keys_ref,
                       *, tq, S, n_valid, topk, n_kv, rep, dh, idx_h, idx_d, idx_bits):
    wi = wi_ref[0]
    qi = qi_ref[0]
    kt = ki_ref[0]
    sc = jnp.zeros((tq, S), F32)
    for h in range(idx_h):
        sc = sc + wi[:, h:h + 1] * jnp.maximum(_dot_nt(qi[:, h * idx_d:(h + 1) * idx_d], kt), 0.0)
    sc = sc * (idx_d ** -0.5)
    col = lax.broadcasted_iota(I32, (tq, S), 1)
    valid = col < n_valid
    keys_ref[...] = _sortable_key(jnp.where(valid, sc, -jnp.inf))
    thr = _topk_threshold(keys_ref, S // LANES, LANES, tq, topk, idx_bits)[:, :1]
    madd = jnp.where(keys_ref[...] >= thr, jnp.where(valid, 0.0, NEG), NEG)
    madd = jnp.concatenate([madd] * rep, axis=0)
    q = q_ref[0]
    for g in range(n_kv):
        qg = jnp.concatenate([q[:, (g * rep + r) * dh:(g * rep + r + 1) * dh] for r in range(rep)], axis=0)
        s = _dot_nt(qg, k_ref[0, :, g * dh:(g + 1) * dh]) + bias_ref[g] + madd
        p = jnp.exp2(s - jnp.max(s, axis=1, keepdims=True))
        out = _dot(p.astype(BF16), v_ref[0, :, g * dh:(g + 1) * dh]) / jnp.sum(p, axis=1, keepdims=True)
        for r in range(rep):
            hcol = (g * rep + r) * dh
            o_ref[0, :, hcol:hcol + dh] = out[r * tq:(r + 1) * tq].astype(o_ref.dtype)


def _dsa_sample(qa, wi, ki_all, k_all, v_all, rel_bias, *, past, n_heads, n_kv, dh, idx_h, idx_d):
    B, tq, _ = qa.shape
    S = k_all.shape[1]
    n_valid = past + tq
    topk = min(TOPK_MAX, n_valid // 4)
    rep = n_heads // n_kv
    qd = n_heads * dh
    idd = idx_h * idx_d
    rel = np.minimum(np.arange(S), n_valid - 1)[None, :] - (past + np.arange(tq))[:, None]
    bias = _bias_lookup(_t5_bucket_np(rel), rel_bias) * LOG2E
    bias = jnp.transpose(bias.reshape(tq, S, n_kv, rep), (2, 3, 0, 1)).reshape(n_kv, rep * tq, S)
    kern = functools.partial(_dsa_sample_kernel, tq=tq, S=S, n_valid=n_valid, topk=topk, n_kv=n_kv, rep=rep,
                             dh=dh, idx_h=idx_h, idx_d=idx_d, idx_bits=max(1, (S - 1).bit_length()))
    return pl.pallas_call(
        kern,
        grid=(B,),
        in_specs=[pl.BlockSpec((1, tq, qd), lambda b: (b, 0, 0)),
                  pl.BlockSpec((1, tq, idd), lambda b: (b, 0, qd // idd)),
                  pl.BlockSpec((1, tq, idx_h), lambda b: (b, 0, 0)),
                  pl.BlockSpec((1, S, idx_d), lambda b: (b, 0, 0)),
                  pl.BlockSpec((1, S, n_kv * dh), lambda b: (b, 0, 0)),
                  pl.BlockSpec((1, S, n_kv * dh), lambda b: (b, 0, 0)),
                  pl.BlockSpec((n_kv, rep * tq, S), lambda b: (0, 0, 0))],
        out_specs=pl.BlockSpec((1, tq, qd), lambda b: (b, 0, 0)),
        out_shape=jax.ShapeDtypeStruct((B, tq, qd), BF16),
        scratch_shapes=[pltpu.VMEM((tq, S), I32)],
        compiler_params=_cparams("parallel"),
        name="dsa_sample",
    )(qa, qa, wi, ki_all, k_all, v_all, bias)


def _dsa_tail_epilogue(accs, extras, j, *, idx_d, idx_h):
    a = accs[0]
    ki = _rms(a[:, :idx_d], extras[0][:, :idx_d])
    return [jnp.concatenate([ki, a[:, idx_d:] * idx_h ** -0.5], axis=1)]


def _dsa_layer(h3, cache, gain, w_qa, w_kv, w_tail, g_q, g_k, g_ki, w_out, rel_bias,
               *, n_heads, n_kv, dh, idx_h, idx_d):
    B, T, D = h3.shape
    M = B * T
    qd = n_heads * dh
    kvd = n_kv * dh
    h2 = h3.reshape(M, D)
    tn = 512
    g_qa = jnp.concatenate([jnp.tile(g_q, n_heads), jnp.ones((idx_h * idx_d,), F32)]).reshape(1, -1)
    qa = _mm(h2, [w_qa], functools.partial(_ep_headnorm, dh=dh, n_norm_tiles=qd // tn,
                                           post_scale=dh ** -0.5 * LOG2E), [BF16],
             gain=gain, extras=[g_qa], tn=tn, name="dsa_q_proj")[0].reshape(B, T, -1)
    g_kv = jnp.concatenate([jnp.tile(g_k, n_kv), jnp.ones((kvd,), F32)]).reshape(1, -1)
    kv = _mm(h2, [w_kv], functools.partial(_ep_headnorm, dh=dh, n_norm_tiles=kvd // tn), [F32],
             gain=gain, extras=[g_kv], tn=tn, name="dsa_kv_proj")[0]
    g_tail = jnp.concatenate([g_ki, jnp.ones((idx_h,), F32)]).reshape(1, -1)
    tail = _mm(h2, [w_tail], functools.partial(_dsa_tail_epilogue, idx_d=idx_d, idx_h=idx_h), [F32],
               gain=gain, extras=[g_tail], name="dsa_idx_proj")[0]
    k_new = kv[:, :kvd].reshape(B, T, kvd)
    v_new = kv[:, kvd:].reshape(B, T, kvd)
    ki_new = tail[:, :idx_d].reshape(B, T, idx_d)
    wi = tail[:, idx_d:].reshape(B, T, idx_h)
    dims = dict(n_heads=n_heads, n_kv=n_kv, dh=dh, idx_h=idx_h, idx_d=idx_d)
    if cache is None:
        att = _dsa_prompt(qa, wi, ki_new.astype(BF16), k_new.astype(BF16), v_new.astype(BF16), rel_bias, **dims)
    else:
        ck, cv, cki = cache
        past = ck.shape[1]
        S = -(-(past + T) // LANES) * LANES
        padr = ((0, 0), (0, S - past - T), (0, 0))
        k_all = jnp.pad(jnp.concatenate([ck.reshape(B, past, kvd), k_new], axis=1).astype(BF16), padr)
        v_all = jnp.pad(jnp.concatenate([cv.reshape(B, past, kvd), v_new], axis=1).astype(BF16), padr)
        ki_all = jnp.pad(jnp.concatenate([cki, ki_new], axis=1).astype(BF16), padr)
        att = _dsa_sample(qa, wi, ki_all, k_all, v_all, rel_bias, past=past, **dims)
    h_new = _mm(att.reshape(M, qd), [w_out], _ep_residual, [F32], extras=[h2], name="dsa_out_proj")[0]
    return (h_new.reshape(B, T, D), k_new.reshape(B, T, n_kv, dh), v_new.reshape(B, T, n_kv, dh), ki_new)


def _router_kernel(h_ref, g_ref, wr_ref, xn_ref, ro_ref, *, n_exp):
    x = _rms(h_ref[...], g_ref[...])
    xn_ref[...] = x.astype(BF16)
    logits = _dot_hi(x, wr_ref[...])
    lane = lax.broadcasted_iota(I32, logits.shape, 1).astype(F32)
    lg = jnp.where(lane < n_exp, logits, -jnp.inf)
    m1 = jnp.max(lg, axis=1, keepdims=True)
    i1 = jnp.min(jnp.where(lg == m1, lane, float(LANES)), axis=1, keepdims=True)
    lg2 = jnp.where(lane == i1, -jnp.inf, lg)
    m2 = jnp.max(lg2, axis=1, keepdims=True)
    i2 = jnp.min(jnp.where(lg2 == m2, lane, float(LANES)), axis=1, keepdims=True)
    e = jnp.exp(m2 - m1)
    g1 = 1.0 / (1.0 + e)
    g2 = e / (1.0 + e)
    ro_ref[...] = jnp.where(lane == 0, i1, jnp.where(lane == 1, i2, jnp.where(lane == 2, g1,
                            jnp.where(lane == 3, g2, 0.0))))


def _router(h2, gain, w_router, *, tm=512):
    M, D = h2.shape
    n_exp = w_router.shape[1]
    tm = _pick_tile(M, tm)
    wr = jnp.pad(w_router.astype(F32), ((0, 0), (0, LANES - n_exp)))
    return pl.pallas_call(
        functools.partial(_router_kernel, n_exp=n_exp),
        grid=(M // tm,),
        in_specs=[pl.BlockSpec((tm, D), lambda i: (i, 0)), pl.BlockSpec((1, D), lambda i: (0, 0)),
                  pl.BlockSpec((D, LANES), lambda i: (0, 0))],
        out_specs=[pl.BlockSpec((tm, D), lambda i: (i, 0)), pl.BlockSpec((tm, LANES), lambda i: (i, 0))],
        out_shape=[jax.ShapeDtypeStruct((M, D), BF16), jax.ShapeDtypeStruct((M, LANES), F32)],
        compiler_params=_cparams("parallel"),
        name="moe_router",
    )(h2, gain.reshape(1, D).astype(F32), wr)


def _moe(h_list, gain, w_router, wg, wu, wd, *, bm=512):
    n_exp = w_router.shape[1]
    routed = [_router(h, gain, w_router) for h in h_list]
    xn = jnp.concatenate([r[0] for r in routed], axis=0)
    ro = jnp.concatenate([r[1] for r in routed], axis=0)
    N, D = xn.shape
    A = N * MOE_TOP_K
    flat_e = ro[:, :MOE_TOP_K].astype(I32).reshape(-1)
    gates = ro[:, MOE_TOP_K:2 * MOE_TOP_K]
    onehot = (flat_e[:, None] == jnp.arange(n_exp, dtype=I32)[None, :]).astype(I32)
    csum = jnp.cumsum(onehot, axis=0)
    rank = jnp.take_along_axis(csum, flat_e[:, None], axis=1)[:, 0] - 1
    counts = csum[-1]
    padded = (counts + bm - 1) // bm * bm
    pad_end = jnp.cumsum(padded)
    dest = (pad_end - padded)[flat_e] + rank
    n_blocks = (A + n_exp * (bm - 1)) // bm
    row_src = jnp.zeros((n_blocks * bm,), I32).at[dest].set(jnp.arange(A, dtype=I32) // MOE_TOP_K)
    block_e = jnp.minimum(jnp.searchsorted(pad_end, jnp.arange(n_blocks, dtype=I32) * bm, side='right'),
                          n_exp - 1).astype(I32)
    n_used = (pad_end[-1:] // bm).astype(I32)
    xs = jnp.take(xn, row_src, axis=0)
    yp = _ffn_experts(xs, block_e, n_used, wg, wu, wd, bm=bm)
    y = jnp.sum(jnp.take(yp, dest, axis=0).reshape(N, MOE_TOP_K, D) * gates[..., None], axis=1)
    outs = []
    off = 0
    for h in h_list:
        outs.append(h + y[off:off + h.shape[0]])
        off += h.shape[0]
    return outs


def kernel(x_prompt, x_sample, state_gdn, state_gdn_conv, cache_dsa_k, cache_dsa_v, cache_dsa_kidx, p_prompt, p_sample, g_mix, g_ffn, g_ple, w_ple_in, w_ple_gate, w_in_gdn, conv_w_gdn, a_log_gdn, dt_bias_gdn, g_onorm_gdn, w_out_gdn, w_in_dsa, g_qnorm_dsa, g_knorm_dsa, g_kidx_norm_dsa, w_out_dsa, rel_bias, w_gate_ffn, w_up_ffn, w_down_ffn, w_router, w_gate_moe, w_up_moe, w_down_moe):
    depth = g_mix.shape[0]
    Bp, Tp, D = x_prompt.shape
    Bs, Ts, _ = x_sample.shape
    nv = a_log_gdn.shape[1]
    dk, dv = state_gdn.shape[3], state_gdn.shape[4]
    conv_dim = conv_w_gdn.shape[2]
    nk = (conv_dim - nv * dv) // (2 * dk)
    gdn_dims = dict(nk=nk, nv=nv, dk=dk, dv=dv)
    n_heads = rel_bias.shape[1]
    n_kv, dh = cache_dsa_k.shape[3], cache_dsa_k.shape[4]
    idx_d = cache_dsa_kidx.shape[3]
    qd, kvd = n_heads * dh, n_kv * dh
    idx_h = (w_in_dsa.shape[2] - qd - 2 * kvd - idx_d) // (idx_d + 1)
    dsa_dims = dict(n_heads=n_heads, n_kv=n_kv, dh=dh, idx_h=idx_h, idx_d=idx_d)

    hp, hs = x_prompt, x_sample
    outs = {k: [] for k in ("sp", "cp", "ss", "cs", "kp", "vp", "kip", "ks", "vs", "kis")}
    for i in range(depth):
        j = i // 2
        if i % 2 == 0:
            w_in = w_in_gdn[j]
            n_main = conv_dim + nv * dv
            wa = (g_mix[i], w_in[:, :n_main].astype(BF16), w_in[:, n_main:].astype(BF16), conv_w_gdn[j],
                  a_log_gdn[j], dt_bias_gdn[j], g_onorm_gdn[j], w_out_gdn[j].astype(BF16))
            hp, cbp, sp = _gdn_layer(hp, None, None, *wa, **gdn_dims)
            hs, cbs, ss = _gdn_layer(hs, state_gdn_conv[j], state_gdn[j], *wa, **gdn_dims)
            outs["sp"].append(sp); outs["cp"].append(cbp); outs["ss"].append(ss); outs["cs"].append(cbs)
            wf = (g_ffn[i], w_gate_ffn[j].astype(BF16), w_up_ffn[j].astype(BF16), w_down_ffn[j].astype(BF16))
            hp = _ffn_dense(hp.reshape(Bp * Tp, D), *wf).reshape(Bp, Tp, D)
            hs = _ffn_dense(hs.reshape(Bs * Ts, D), *wf).reshape(Bs, Ts, D)
        else:
            w_in = w_in_dsa[j]
            o1 = qd + 2 * kvd
            o2 = o1 + idx_h * idx_d
            w_qa = jnp.concatenate([w_in[:, :qd], w_in[:, o1:o2]], axis=1).astype(BF16)
            wb = (g_mix[i], w_qa, w_in[:, qd:o1].astype(BF16), w_in[:, o2:].astype(BF16), g_qnorm_dsa[j],
                  g_knorm_dsa[j], g_kidx_norm_dsa[j], w_out_dsa[j].astype(BF16), rel_bias)
            hp, k, v, ki = _dsa_layer(hp, None, *wb, **dsa_dims)
            outs["kp"].append(k); outs["vp"].append(v); outs["kip"].append(ki)
            hs, k, v, ki = _dsa_layer(hs, (cache_dsa_k[j], cache_dsa_v[j], cache_dsa_kidx[j]), *wb, **dsa_dims)
            outs["ks"].append(k); outs["vs"].append(v); outs["kis"].append(ki)
            hp2, hs2 = _moe([hp.reshape(Bp * Tp, D), hs.reshape(Bs * Ts, D)], g_ffn[i], w_router[j],
                            w_gate_moe[j].astype(BF16), w_up_moe[j].astype(BF16), w_down_moe[j].astype(BF16))
            hp, hs = hp2.reshape(Bp, Tp, D), hs2.reshape(Bs, Ts, D)
        wpl = (g_ple[i], w_ple_in[i].astype(BF16), w_ple_gate[i].astype(BF16))
        hp = _ple_add(hp.reshape(Bp * Tp, D), p_prompt[i].reshape(Bp * Tp, -1), *wpl).reshape(Bp, Tp, D)
        hs = _ple_add(hs.reshape(Bs * Ts, D), p_sample[i].reshape(Bs * Ts, -1), *wpl).reshape(Bs, Ts, D)
    st = lambda name: jnp.stack(outs[name])
    return (hp, hs, st("sp"), st("cp"), st("kp"), st("vp"), st("kip"),
            st("ss"), st("cs"), st("ks"), st("vs"), st("kis"))
```

```python
import functools
import math

import numpy as np
import jax
import jax.numpy as jnp
from jax import lax
from jax.experimental import pallas as pl
from jax.experimental.pallas import tpu as pltpu

F32 = jnp.float32
BF16 = jnp.bfloat16
I32 = jnp.int32

EPS = 1e-6
CHUNK = 64
TOPK_MAX = 256
MOE_TOP_K = 2
NUM_BUCKETS = 32
MAX_DISTANCE = 128
NEG = -1e30
INT_MIN = -(2 ** 31)
LOG2E = math.log2(math.e)
HI = lax.Precision.HIGHEST

V7X_VMEM_LIMIT = 56 * 1024 * 1024
LANES = 128


def _cparams(*sem):
    return pltpu.CompilerParams(dimension_semantics=sem, vmem_limit_bytes=V7X_VMEM_LIMIT)


def _sigmoid(x):
    return 1.0 / (1.0 + jnp.exp(-x))


def _silu(x):
    return x * _sigmoid(x)


def _rms(x, g):
    return x * lax.rsqrt(jnp.mean(x * x, axis=-1, keepdims=True) + EPS) * g


def _dot(a, b):
    return jnp.dot(a, b, preferred_element_type=F32)


def _dot_nt(a, b):
    return lax.dot_general(a, b, (((1,), (1,)), ((), ())), preferred_element_type=F32)


def _dot_tn(a, b):
    return lax.dot_general(a, b, (((0,), (0,)), ((), ())), preferred_element_type=F32)


def _dot_hi(a, b):
    return jnp.dot(a, b, preferred_element_type=F32, precision=HI)


def _pick_tile(n, pref):
    t = min(n, pref)
    while n % t:
        t //= 2
    return t


def _mm_kernel(*refs, has_gain, n_w, n_extra, n_out, epilogue):
    x_ref = refs[0]
    pos = 1
    g_ref = None
    if has_gain:
        g_ref = refs[pos]
        pos += 1
    w_refs = refs[pos:pos + n_w]
    pos += n_w
    e_refs = refs[pos:pos + n_extra]
    pos += n_extra
    o_refs = refs[pos:pos + n_out]
    pos += n_out
    xs_ref = refs[pos]

    @pl.when(pl.program_id(1) == 0)
    def _():
        x = x_ref[...].astype(F32)
        if has_gain:
            x = _rms(x, g_ref[...])
        xs_ref[...] = x.astype(BF16)

    xs = xs_ref[...]
    accs = [_dot(xs, w[...]) for w in w_refs]
    outs = epilogue(accs, [e[...] for e in e_refs], pl.program_id(1))
    for o_ref, o in zip(o_refs, outs):
        o_ref[...] = o.astype(o_ref.dtype)


def _mm(x, ws, epilogue, out_dtypes, *, gain=None, extras=(), tm=1024, tn=512, name=None):
    M, K = x.shape
    N = ws[0].shape[1]
    tm = _pick_tile(M, tm)
    tn = _pick_tile(N, tn)
    in_specs = [pl.BlockSpec((tm, K), lambda i, j: (i, 0))]
    args = [x]
    if gain is not None:
        in_specs.append(pl.BlockSpec((1, K), lambda i, j: (0, 0)))
        args.append(gain.reshape(1, K).astype(F32))
    for w in ws:
        in_specs.append(pl.BlockSpec((K, tn), lambda i, j: (0, j)))
        args.append(w)
    for e in extras:
        if e.shape[0] == 1:
            in_specs.append(pl.BlockSpec((1, tn), lambda i, j: (0, j)))
        else:
            in_specs.append(pl.BlockSpec((tm, tn), lambda i, j: (i, j)))
        args.append(e)
    out_shape = [jax.ShapeDtypeStruct((M, N), d) for d in out_dtypes]
    out_specs = [pl.BlockSpec((tm, tn), lambda i, j: (i, j)) for _ in out_dtypes]
    kern = functools.partial(_mm_kernel, has_gain=gain is not None, n_w=len(ws), n_extra=len(extras),
                             n_out=len(out_dtypes), epilogue=epilogue)
    return pl.pallas_call(
        kern,
        grid=(M // tm, N // tn),
        in_specs=in_specs,
        out_specs=out_specs,
        out_shape=out_shape,
        scratch_shapes=[pltpu.VMEM((tm, K), BF16)],
        compiler_params=_cparams("parallel", "arbitrary"),
        name=name,
    )(*args)


def _ep_plain(accs, extras, j):
    return [accs[0]]


def _ep_residual(accs, extras, j):
    return [extras[0] + accs[0]]


def _ep_headnorm(accs, extras, j, *, dh, n_norm_tiles, post_scale=1.0):
    a = accs[0]
    g = extras[0]
    cols = []
    for h in range(a.shape[1] // dh):
        sl = slice(h * dh, (h + 1) * dh)
        cols.append(_rms(a[:, sl], g[:, sl]) * post_scale)
    normed = jnp.concatenate(cols, axis=1) if len(cols) > 1 else cols[0]
    return [jnp.where(j < n_norm_tiles, normed, a)]


def _ffn_kernel(*refs, has_gain, has_be):
    if has_be:
        used = pl.program_id(0) < refs[1][0]
        pl.when(used)(lambda: _ffn_body(refs[2:], has_gain))
        o_ref = refs[6]

        @pl.when(jnp.logical_not(used))
        def _():
            o_ref[...] = jnp.zeros_like(o_ref)
    else:
        _ffn_body(refs, has_gain)


def _ffn_body(refs, has_gain):
    pos = 0
    x_ref = refs[pos]
    pos += 1
    g_ref = None
    if has_gain:
        g_ref = refs[pos]
        pos += 1
    wg_ref, wu_ref, wd_ref, o_ref, xs_ref, acc_ref = refs[pos:pos + 6]
    j = pl.program_id(1)

    @pl.when(j == 0)
    def _():
        if has_gain:
            x = x_ref[...].astype(F32)
            xs_ref[...] = _rms(x, g_ref[...]).astype(BF16)
            acc_ref[...] = x
        else:
            xs_ref[...] = x_ref[...].astype(BF16)
            acc_ref[...] = jnp.zeros_like(acc_ref)

    xs = xs_ref[...]
    wg = wg_ref[...].reshape(wg_ref.shape[-2:])
    wu = wu_ref[...].reshape(wu_ref.shape[-2:])
    wd = wd_ref[...].reshape(wd_ref.shape[-2:])
    a = (_silu(_dot(xs, wg)) * _dot(xs, wu)).astype(BF16)
    acc_ref[...] += _dot(a, wd)

    @pl.when(j == pl.num_programs(1) - 1)
    def _():
        o_ref[...] = acc_ref[...].astype(o_ref.dtype)


def _ffn_dense(h, gain, wg, wu, wd, *, tm=512, tf=512):
    M, D = h.shape
    Fdim = wg.shape[1]
    tm = _pick_tile(M, tm)
    tf = _pick_tile(Fdim, tf)
    return pl.pallas_call(
        functools.partial(_ffn_kernel, has_gain=True, has_be=False),
        grid=(M // tm, Fdim // tf),
        in_specs=[pl.BlockSpec((tm, D), lambda i, j: (i, 0)),
                  pl.BlockSpec((1, D), lambda i, j: (0, 0)),
                  pl.BlockSpec((D, tf), lambda i, j: (0, j)),
                  pl.BlockSpec((D, tf), lambda i, j: (0, j)),
                  pl.BlockSpec((tf, D), lambda i, j: (j, 0))],
        out_specs=pl.BlockSpec((tm, D), lambda i, j: (i, 0)),
        out_shape=jax.ShapeDtypeStruct((M, D), F32),
        scratch_shapes=[pltpu.VMEM((tm, D), BF16), pltpu.VMEM((tm, D), F32)],
        compiler_params=_cparams("parallel", "arbitrary"),
        name="ffn_dense",
    )(h, gain.reshape(1, D).astype(F32), wg, wu, wd)


def _ffn_experts(xs, block_e, n_used, wg, wu, wd, *, bm, tf=512):
    P, D = xs.shape
    Fdim = wg.shape[2]
    tf = _pick_tile(Fdim, tf)
    nf = Fdim // tf

    def row(i, nu):
        return jnp.minimum(i, nu[0] - 1)

    def col(i, j, nu):
        return jnp.where(i < nu[0], j, nf - 1)

    grid_spec = pltpu.PrefetchScalarGridSpec(
        num_scalar_prefetch=2,
        grid=(P // bm, nf),
        in_specs=[pl.BlockSpec((bm, D), lambda i, j, be, nu: (row(i, nu), 0)),
                  pl.BlockSpec((1, D, tf), lambda i, j, be, nu: (be[row(i, nu)], 0, col(i, j, nu))),
                  pl.BlockSpec((1, D, tf), lambda i, j, be, nu: (be[row(i, nu)], 0, col(i, j, nu))),
                  pl.BlockSpec((1, tf, D), lambda i, j, be, nu: (be[row(i, nu)], col(i, j, nu), 0))],
        out_specs=pl.BlockSpec((bm, D), lambda i, j, be, nu: (i, 0)),
        scratch_shapes=[pltpu.VMEM((bm, D), BF16), pltpu.VMEM((bm, D), F32)],
    )
    return pl.pallas_call(
        functools.partial(_ffn_kernel, has_gain=False, has_be=True),
        grid_spec=grid_spec,
        out_shape=jax.ShapeDtypeStruct((P, D), BF16),
        compiler_params=_cparams("parallel", "arbitrary"),
        name="moe_experts",
    )(block_e, n_used, xs, wg, wu, wd)


def _ple_kernel(*refs, has_moe, tn):
    if has_moe:
        h_ref, y0_ref, y1_ref, ro_ref, g_ref, wgate_ref, p_ref, wp_ref, o_ref, xs_ref, hs_ref = refs
    else:
        h_ref, g_ref, wgate_ref, p_ref, wp_ref, o_ref, xs_ref, hs_ref = refs
    j = pl.program_id(1)

    @pl.when(j == 0)
    def _():
        h = h_ref[...]
        if has_moe:
            ro = ro_ref[...]
            h = h + (ro[:, MOE_TOP_K:MOE_TOP_K + 1] * y0_ref[...].astype(F32)
                     + ro[:, MOE_TOP_K + 1:MOE_TOP_K + 2] * y1_ref[...].astype(F32))
        hs_ref[...] = h
        xs_ref[...] = _rms(h, g_ref[...]).astype(BF16)

    gate = _sigmoid(_dot(xs_ref[...], wgate_ref[...]))
    ht = hs_ref[:, pl.ds(pl.multiple_of(j * tn, tn), tn)]
    o_ref[...] = ht + gate * _dot(p_ref[...].astype(BF16), wp_ref[...])


def _ple_add(h, p, gain, wp, wgate, *, moe=None, tm=1024, tn=512):
    M, D = h.shape
    Pd = p.shape[1]
    tm = _pick_tile(M, tm if moe is None else 512)
    tn = _pick_tile(D, tn)
    in_specs = [pl.BlockSpec((tm, D), lambda i, j: (i, 0))]
    args = [h]
    if moe is not None:
        y0, y1, ro, first_row = moe
        assert first_row % tm == 0
        off = first_row // tm
        in_specs += [pl.BlockSpec((tm, D), lambda i, j: (i + off, 0)),
                     pl.BlockSpec((tm, D), lambda i, j: (i + off, 0)),
                     pl.BlockSpec((tm, LANES), lambda i, j: (i + off, 0))]
        args += [y0, y1, ro]
    in_specs += [pl.BlockSpec((1, D), lambda i, j: (0, 0)),
                 pl.BlockSpec((D, tn), lambda i, j: (0, j)),
                 pl.BlockSpec((tm, Pd), lambda i, j: (i, 0)),
                 pl.BlockSpec((Pd, tn), lambda i, j: (0, j))]
    args += [gain.reshape(1, D).astype(F32), wgate, p, wp]
    return pl.pallas_call(
        functools.partial(_ple_kernel, has_moe=moe is not None, tn=tn),
        grid=(M // tm, D // tn),
        in_specs=in_specs,
        out_specs=pl.BlockSpec((tm, tn), lambda i, j: (i, j)),
        out_shape=jax.ShapeDtypeStruct((M, D), F32),
        scratch_shapes=[pltpu.VMEM((tm, D), BF16), pltpu.VMEM((tm, D), F32)],
        compiler_params=_cparams("parallel", "arbitrary"),
        name="ple_add",
    )(*args)


def _gdn_gate_epilogue(accs, extras, j, *, nv):
    ba = accs[0]
    a_log, dt = extras
    beta = _sigmoid(ba[:, :nv])
    x = ba[:, nv:] + dt[:, nv:]
    softplus = jnp.maximum(x, 0.0) + jnp.log(1.0 + jnp.exp(-jnp.abs(x)))
    g = -jnp.exp(a_log[:, nv:]) * softplus
    return [jnp.concatenate([beta, g], axis=1)]


def _gdn_conv_kernel(x_ref, buf_ref, w_ref, y_ref, cs_ref, xe_ref, *, tt, tc, dk, n_q_blocks, n_qk_blocks):
    cb = pl.program_id(1)
    t = pl.program_id(2)

    @pl.when(t == 0)
    def _():
        xe_ref[0:8, :] = buf_ref[0]

    @pl.when(t > 0)
    def _():
        xe_ref[0:8, :] = xe_ref[tt:tt + 8, :]

    xe_ref[8:8 + tt, :] = x_ref[0].astype(F32)
    w = w_ref[...]
    acc = (w[3:4] * xe_ref[8:8 + tt, :] + w[2:3] * xe_ref[7:7 + tt, :]
           + w[1:2] * xe_ref[6:6 + tt, :] + w[0:1] * xe_ref[5:5 + tt, :])
    y = _silu(acc)
    cs_ref[0] = xe_ref[tt:tt + 8, :]

    @pl.when(cb < n_qk_blocks)
    def _():
        scale = jnp.where(cb < n_q_blocks, dk ** -0.5, 1.0).astype(F32)
        for h in range(tc // dk):
            yh = y[:, h * dk:(h + 1) * dk]
            y_ref[0, :, h * dk:(h + 1) * dk] = (
                yh * lax.rsqrt(jnp.sum(yh * yh, axis=-1, keepdims=True) + EPS) * scale).astype(y_ref.dtype)

    @pl.when(cb >= n_qk_blocks)
    def _():
        y_ref[0] = y.astype(y_ref.dtype)


def _gdn_conv(qkvz, buf8, conv_w, *, conv_dim, qk_dim, dk, tt=256, tc=1024):
    B, T, _ = qkvz.shape
    tt = _pick_tile(T, tt)
    assert tt >= 8 and conv_dim % tc == 0 and qk_dim % tc == 0
    kern = functools.partial(_gdn_conv_kernel, tt=tt, tc=tc, dk=dk, n_q_blocks=qk_dim // tc,
                             n_qk_blocks=2 * qk_dim // tc)
    return pl.pallas_call(
        kern,
        grid=(B, conv_dim // tc, T // tt),
        in_specs=[pl.BlockSpec((1, tt, tc), lambda b, c, t: (b, t, c)),
                  pl.BlockSpec((1, 8, tc), lambda b, c, t: (b, 0, c)),
                  pl.BlockSpec((4, tc), lambda b, c, t: (0, c))],
        out_specs=[pl.BlockSpec((1, tt, tc), lambda b, c, t: (b, t, c)),
                   pl.BlockSpec((1, 8, tc), lambda b, c, t: (b, 0, c))],
        out_shape=[jax.ShapeDtypeStruct((B, T, conv_dim), BF16),
                   jax.ShapeDtypeStruct((B, 8, conv_dim), F32)],
        scratch_shapes=[pltpu.VMEM((tt + 8, tc), F32)],
        compiler_params=_cparams("parallel", "parallel", "arbitrary"),
        name="gdn_conv",
    )(qkvz, buf8, conv_w)


def _dot_b(a, b):
    return _dot(a.astype(BF16), b.astype(BF16))


def _gdn_scan_kernel(q_ref, k_ref, v_ref, z_ref, gc_ref, gr_ref, *rest, C, cps, Gv, dk, dv, rep, has_s0):
    if has_s0:
        s0_ref, gon_ref, o_ref, sN_ref, S_scr = rest
    else:
        gon_ref, o_ref, sN_ref, S_scr = rest
    t = pl.program_id(2)

    @pl.when(t == 0)
    def _():
        if has_s0:
            S_scr[...] = s0_ref[0]
        else:
            S_scr[...] = jnp.zeros_like(S_scr)

    rows = lax.broadcasted_iota(I32, (C, C), 0)
    cols = lax.broadcasted_iota(I32, (C, C), 1)
    incl = rows >= cols
    strict = rows > cols
    ltri = jnp.where(incl, 1.0, 0.0).astype(F32)
    eye = jnp.where(rows == cols, 1.0, 0.0).astype(F32)
    gon = gon_ref[...]

    chains = [(c, hh) for c in range(cps) for hh in range(Gv)]
    gcm, gcum_col, gcum_row = [], [], []
    for c in range(cps):
        r0 = c * C
        m = gc_ref[0, 0, r0:r0 + C, :]
        gcm.append(m)
        gcum_col.append(_dot_hi(ltri, m[:, Gv:]))
        gcum_row.append(lax.dot_general(gr_ref[0, 0, Gv:, r0:r0 + C], ltri, (((1,), (1,)), ((), ())),
                                        preferred_element_type=F32, precision=HI))

    st = []
    for (c, hh) in chains:
        r0 = c * C
        kh = hh // rep
        qh = q_ref[0, r0:r0 + C, kh * dk:(kh + 1) * dk].astype(F32)
        kk = k_ref[0, r0:r0 + C, kh * dk:(kh + 1) * dk].astype(F32)
        vh = v_ref[0, r0:r0 + C, hh * dv:(hh + 1) * dv].astype(F32)
        beta = gcm[c][:, hh:hh + 1]
        gcol = gcum_col[c][:, hh:hh + 1]
        grow = gcum_row[c][hh:hh + 1, :]
        glast = grow[:, C - 1:C]
        decay = jnp.exp(jnp.where(incl, gcol - grow, NEG))
        eg = jnp.exp(gcol)
        kb = kk * beta
        kq = _dot_nt(jnp.concatenate([kb, qh], axis=0).astype(BF16), kk.astype(BF16))
        p = jnp.where(strict, -(kq[:C] * decay), 0.0)
        st.append(dict(
            p=p, y=eye + p, qk=(kq[C:] * decay).astype(BF16),
            rhs=jnp.concatenate([vh * beta, kb * eg], axis=1),
            qe=qh * eg, kdT=jnp.transpose(kk * jnp.exp(glast - gcol)).astype(BF16),
            egl=jnp.exp(glast)))

    for s in st:
        s["p"] = _dot_b(s["p"], s["p"])
    m = 4
    while m < C:
        for s in st:
            r = _dot_b(jnp.concatenate([s["p"], s["y"]], axis=0), s["p"])
            s["p"] = r[:C]
            s["y"] = s["y"] + r[C:]
        m *= 2
    for s in st:
        s["y"] = s["y"] + _dot_b(s["y"], s["p"])
    for s in st:
        sol = _dot_b(s["y"], s["rhs"])
        s["u"] = sol[:, :dv]
        s["wq"] = jnp.concatenate([sol[:, dv:], s["qe"]], axis=0).astype(BF16)
        s["lhs2"] = jnp.concatenate([s["qk"], s["kdT"]], axis=0)

    S = [S_scr[hh] for hh in range(Gv)]
    for c in range(cps):
        r0 = c * C
        cur = [st[c * Gv + hh] for hh in range(Gv)]
        ws = [_dot(s["wq"], S[hh].astype(BF16)) for hh, s in enumerate(cur)]
        v_new = [(s["u"] - ws[hh][:C]).astype(BF16) for hh, s in enumerate(cur)]
        r2 = [_dot(s["lhs2"], v_new[hh]) for hh, s in enumerate(cur)]
        for hh, s in enumerate(cur):
            o = ws[hh][C:] + r2[hh][:C]
            S[hh] = S[hh] * s["egl"] + r2[hh][C:]
            zz = z_ref[0, r0:r0 + C, hh * dv:(hh + 1) * dv].astype(F32)
            o_ref[0, r0:r0 + C, hh * dv:(hh + 1) * dv] = (_rms(o, gon) * _silu(zz)).astype(o_ref.dtype)
    for hh in range(Gv):
        S_scr[hh] = S[hh]

    @pl.when(t == pl.num_programs(2) - 1)
    def _():
        sN_ref[0] = S_scr[...]


def _gdn_scan(y, qkvz, gates, s0, g_onorm, *, nk, nv, dk, dv, Gv=8):
    B, T, conv_dim = y.shape
    rep = nv // nk
    C = min(CHUNK, T)
    cps = 1 if C == T else max(1, LANES // C)
    TB = C * cps
    assert T % TB == 0 and Gv % rep == 0 and nv % Gv == 0
    Gk = Gv // rep
    HB = nv // Gv
    beta = gates[..., :nv].reshape(B, T, HB, Gv)
    g = gates[..., nv:].reshape(B, T, HB, Gv)
    gcol = jnp.transpose(jnp.concatenate([beta, g], axis=-1), (0, 2, 1, 3))
    grow = jnp.transpose(gcol, (0, 1, 3, 2))
    k_off = (nk * dk) // (Gk * dk)
    v_off = (2 * nk * dk) // (Gv * dv)
    z_off = conv_dim // (Gv * dv)
    in_specs = [pl.BlockSpec((1, TB, Gk * dk), lambda b, h, t: (b, t, h)),
                pl.BlockSpec((1, TB, Gk * dk), lambda b, h, t: (b, t, k_off + h)),
                pl.BlockSpec((1, TB, Gv * dv), lambda b, h, t: (b, t, v_off + h)),
                pl.BlockSpec((1, TB, Gv * dv), lambda b, h, t: (b, t, z_off + h)),
                pl.BlockSpec((1, 1, TB, 2 * Gv), lambda b, h, t: (b, h, t, 0)),
                pl.BlockSpec((1, 1, 2 * Gv, TB), lambda b, h, t: (b, h, 0, t))]
    args = [y, y, y, qkvz, gcol, grow]
    if s0 is not None:
        in_specs.append(pl.BlockSpec((1, Gv, dk, dv), lambda b, h, t: (b, h, 0, 0)))
        args.append(s0)
    in_specs.append(pl.BlockSpec((1, dv), lambda b, h, t: (0, 0)))
    args.append(g_onorm.reshape(1, dv).astype(F32))
    kern = functools.partial(_gdn_scan_kernel, C=C, cps=cps, Gv=Gv, dk=dk, dv=dv, rep=rep,
                             has_s0=s0 is not None)
    return pl.pallas_call(
        kern,
        grid=(B, HB, T // TB),
        in_specs=in_specs,
        out_specs=[pl.BlockSpec((1, TB, Gv * dv), lambda b, h, t: (b, t, h)),
                   pl.BlockSpec((1, Gv, dk, dv), lambda b, h, t: (b, h, 0, 0))],
        out_shape=[jax.ShapeDtypeStruct((B, T, nv * dv), BF16),
                   jax.ShapeDtypeStruct((B, nv, dk, dv), F32)],
        scratch_shapes=[pltpu.VMEM((Gv, dk, dv), F32)],
        compiler_params=_cparams("parallel", "parallel", "arbitrary"),
        name="gdn_scan",
    )(*args)


def _gdn_layer(h3, conv_buf, s0, gain, w_main, w_ba, conv_w, a_log, dt_bias, g_onorm, w_out, *, nk, nv, dk, dv):
    B, T, D = h3.shape
    M = B * T
    qk_dim = nk * dk
    conv_dim = 2 * qk_dim + nv * dv
    h2 = h3.reshape(M, D)
    qkvz = _mm(h2, [w_main], _ep_plain, [BF16], gain=gain, name="gdn_in_proj")[0].reshape(B, T, -1)
    pad = jnp.zeros((1, nv), F32)
    gates = _mm(h2, [w_ba], functools.partial(_gdn_gate_epilogue, nv=nv), [F32], gain=gain,
                extras=[jnp.concatenate([pad, a_log.reshape(1, nv)], axis=1),
                        jnp.concatenate([pad, dt_bias.reshape(1, nv)], axis=1)],
                name="gdn_gates")[0].reshape(B, T, 2 * nv)
    if conv_buf is None:
        buf8 = jnp.zeros((B, 8, conv_dim), F32)
    else:
        buf8 = jnp.pad(conv_buf, ((0, 0), (8 - conv_buf.shape[1], 0), (0, 0)))
    y, last8 = _gdn_conv(qkvz, buf8, conv_w, conv_dim=conv_dim, qk_dim=qk_dim, dk=dk)
    o, s_new = _gdn_scan(y, qkvz, gates, s0, g_onorm, nk=nk, nv=nv, dk=dk, dv=dv)
    h_new = _mm(o.reshape(M, nv * dv), [w_out], _ep_residual, [F32], extras=[h2], name="gdn_out_proj")[0]
    return h_new.reshape(B, T, D), last8[:, 5:, :], s_new


def _t5_bucket_np(rel):
    rel = np.asarray(rel, np.int32)
    half = NUM_BUCKETS // 2
    exact = half // 2
    base = np.where(rel > 0, half, 0)
    n = np.abs(rel)
    nf = np.maximum(n, 1).astype(np.float32)
    large = exact + (np.log(nf / np.float32(exact)) / np.float32(math.log(MAX_DISTANCE / exact))
                     * np.float32(half - exact)).astype(np.int32)
    large = np.minimum(large, half - 1)
    return base + np.where(n < exact, n, large)


def _lookup_kernel(oh_ref, tab_ref, o_ref):
    o_ref[...] = _dot_hi(oh_ref[...], tab_ref[...])


def _bias_lookup(buckets, rel_bias, far_bucket=None):
    flat = np.asarray(buckets).reshape(-1)
    nb, H = rel_bias.shape
    onehot = np.zeros((flat.size, nb), np.float32)
    onehot[np.arange(flat.size), flat] = 1.0
    if far_bucket is not None:
        onehot[:, far_bucket] -= 1.0
    R_ = flat.size
    tr = _pick_tile(R_, 2048)
    out = pl.pallas_call(
        _lookup_kernel,
        grid=(R_ // tr,),
        in_specs=[pl.BlockSpec((tr, nb), lambda i: (i, 0)), pl.BlockSpec((nb, H), lambda i: (0, 0))],
        out_specs=pl.BlockSpec((tr, H), lambda i: (i, 0)),
        out_shape=jax.ShapeDtypeStruct((R_, H), F32),
        compiler_params=_cparams("parallel"),
    )(jnp.asarray(onehot), rel_bias.astype(F32))
    return out.reshape(tuple(np.asarray(buckets).shape) + (H,))


def _sortable_key(x):
    bits = pltpu.bitcast(x, I32)
    return bits ^ ((bits >> 31) & 0x7FFFFFFF)


def _topk_threshold(keys_ref, n_tiles, tile, rows, topk, idx_bits):
    sub = tile // LANES

    def count(pred):
        def body(c, acc):
            cs = pl.multiple_of(c * tile, tile)
            for s in range(sub):
                kt = keys_ref[:, pl.ds(cs + s * LANES, LANES)]
                col = cs + s * LANES + lax.broadcasted_iota(I32, (rows, LANES), 1)
                acc = acc + pred(kt, col)
            return acc
        acc = lax.fori_loop(0, n_tiles, body, jnp.zeros((rows, LANES), F32))
        return jnp.sum(acc, axis=1, keepdims=True)

    def count_ge(cand):
        return count(lambda kt, col: jnp.where(kt >= cand, 1.0, 0.0))

    kf = float(topk)
    zero = jnp.zeros((rows, LANES), I32)
    lowest = jnp.full((rows, LANES), INT_MIN, I32)
    c_all = count_ge(lowest)
    c_pos = count_ge(zero)
    prefix = jnp.where(c_pos >= kf, zero, lowest)
    n_ge = jnp.where(c_pos >= kf, c_pos, c_all)

    def bit_cond(carry):
        it, _, n_ge = carry
        return jnp.logical_and(it < 31, jnp.max(n_ge) > kf)

    def bit_body(carry):
        it, prefix, n_ge = carry
        cand = prefix | jnp.left_shift(jnp.int32(1), 30 - it)
        cnt = count_ge(cand)
        ok = cnt >= kf
        return it + 1, jnp.where(ok, cand, prefix), jnp.where(ok, cnt, n_ge)

    _, thr, n_ge = lax.while_loop(bit_cond, bit_body, (jnp.int32(0), prefix, n_ge))

    @pl.when(jnp.max(n_ge) > kf)
    def _():
        n_gt = count(lambda kt, col: jnp.where(kt > thr, 1.0, 0.0))
        need = kf - n_gt

        def idx_body(it, p):
            cand = p | jnp.left_shift(jnp.int32(1), idx_bits - 1 - it)
            c = count(lambda kt, col: jnp.where(kt == thr, jnp.where(col < cand, 1.0, 0.0), 0.0))
            return jnp.where(c < need, cand, p)

        p = lax.fori_loop(0, idx_bits, idx_body, zero)

        def demote(c, carry):
            cs = pl.multiple_of(c * tile, tile)
            for s in range(sub):
                sl = pl.ds(cs + s * LANES, LANES)
                kt = keys_ref[:, sl]
                col = cs + s * LANES + lax.broadcasted_iota(I32, (rows, LANES), 1)
                keys_ref[:, sl] = jnp.where(kt == thr, jnp.where(col > p, kt - 1, kt), kt)
            return carry

        lax.fori_loop(0, n_tiles, demote, 0)

    return thr


def _dsa_prompt_kernel(q_ref, qi_ref, wi_ref, ki_ref, k_ref, v_ref, bias_ref, o_ref,
                       keys_ref, thr_ref, m_ref, acc_ref,
                       *, tq, ts, tf, topk, n_kv, rep, dh, idx_h, idx_d, chunk, idx_bits):
    i = pl.program_id(1)
    nvis = (i + 1) * tq
    n_st = (nvis + ts - 1) // ts
    row = lax.broadcasted_iota(I32, (tq, 1), 0)
    limit = i * tq + (row // chunk + 1) * chunk

    wi = wi_ref[0]
    qi = qi_ref[0]
    q_stack = jnp.concatenate([qi[:, h * idx_d:(h + 1) * idx_d] for h in range(idx_h)], axis=0)
    wis = [wi[:, h:h + 1] for h in range(idx_h)]

    def score_body(t, carry):
        ks = pl.multiple_of(t * ts, ts)
        kt = ki_ref[0, pl.ds(ks, ts), :]
        s_all = _dot_nt(q_stack, kt)
        sc = jnp.zeros((tq, ts), F32)
        for h in range(idx_h):
            sc = sc + wis[h] * jnp.maximum(s_all[h * tq:(h + 1) * tq], 0.0)
        sc = sc * (idx_d ** -0.5)
        col = ks + lax.broadcasted_iota(I32, (tq, ts), 1)
        sc = jnp.where(col < limit, sc, -jnp.inf)
        keys_ref[:, pl.ds(ks, ts)] = _sortable_key(sc)
        return carry

    lax.fori_loop(0, n_st, score_body, 0)

    thr_ref[...] = jnp.full((tq, LANES), INT_MIN, I32)

    @pl.when(nvis > topk)
    def _():
        thr_ref[...] = _topk_threshold(keys_ref, n_st, ts, tq, topk, idx_bits)

    thr = thr_ref[:, :1]
    q = q_ref[0]

    qg = [jnp.concatenate([q[:, (g * rep + r) * dh:(g * rep + r + 1) * dh] for r in range(rep)], axis=0)
          for g in range(n_kv)]
    m_ref[...] = jnp.full(m_ref.shape, NEG, F32)
    acc_ref[...] = jnp.zeros(acc_ref.shape, F32)

    def attend(ks, tk, bias_of, col_lo, col_hi):
        col = ks + lax.broadcasted_iota(I32, (tq, tk), 1)
        madd = jnp.where(keys_ref[:, pl.ds(ks, tk)] >= thr, 0.0, NEG)
        if col_lo is not None:
            madd = jnp.where(col >= col_lo, madd, NEG)
        if col_hi is not None:
            madd = jnp.where(col < col_hi, madd, NEG)
        madd = jnp.concatenate([madd] * rep, axis=0)
        ones = jnp.ones((tk, dh), BF16)
        s = []
        for g in range(n_kv):
            sg = _dot_nt(qg[g], k_ref[0, pl.ds(ks, tk), g * dh:(g + 1) * dh]) + madd
            s.append(sg if bias_of is None else sg + bias_of(g))
        for g in range(n_kv):
            m_prev = m_ref[g]
            m_next = jnp.maximum(m_prev, jnp.max(s[g], axis=1, keepdims=True))
            alpha = jnp.exp2(m_prev - m_next)
            p = jnp.exp2(s[g] - jnp.concatenate([m_next] * (tk // LANES), axis=1)).astype(BF16)
            vt = jnp.concatenate([v_ref[0, pl.ds(ks, tk), g * dh:(g + 1) * dh], ones], axis=1)
            acc_ref[g] = jnp.concatenate([alpha] * (2 * dh // LANES), axis=1) * acc_ref[g] + _dot(p, vt)
            m_ref[g] = m_next

    far_end = jnp.maximum(i - 1, 0) * tq

    def far_body(t, carry):
        attend(pl.multiple_of(t * tf, tf), tf, None, None, far_end)
        return carry

    lax.fori_loop(0, (far_end + tf - 1) // tf, far_body, 0)

    @pl.when(i == 0)
    def _():
        attend(0, tq, lambda g: bias_ref[g, :, tq:], None, limit)

    @pl.when(i > 0)
    def _():
        attend(pl.multiple_of((i - 1) * tq, tq), 2 * tq, lambda g: bias_ref[g], None, limit)

    for g in range(n_kv):
        acc = acc_ref[g]
        out = acc[:, :dh] / acc[:, dh:]
        for r in range(rep):
            hcol = (g * rep + r) * dh
            o_ref[0, :, hcol:hcol + dh] = out[r * tq:(r + 1) * tq].astype(o_ref.dtype)


def _dsa_prompt(qa, wi, ki, kb, vb, rel_bias, *, n_heads, n_kv, dh, idx_h, idx_d, tq=128):
    B, T, _ = qa.shape
    assert T % tq == 0 and tq % CHUNK == 0
    topk = min(TOPK_MAX, T // 4)
    rep = n_heads // n_kv
    qd = n_heads * dh
    idd = idx_h * idx_d
    assert qd % idd == 0
    ts = _pick_tile(T, 512)
    far_bucket = int(_t5_bucket_np(np.array([-(tq + 1)]))[0])
    assert np.all(_t5_bucket_np(-np.arange(tq + 1, 4 * T)) == far_bucket)
    dq = np.arange(tq)[:, None]
    buckets = _t5_bucket_np(np.arange(2 * tq)[None, :] - tq - dq)
    bias = _bias_lookup(buckets, rel_bias, far_bucket) * LOG2E
    bias = jnp.transpose(bias.reshape(tq, 2 * tq, n_kv, rep), (2, 3, 0, 1)).reshape(n_kv, rep * tq, 2 * tq)
    kern = functools.partial(_dsa_prompt_kernel, tq=tq, ts=ts, tf=ts, topk=topk, n_kv=n_kv, rep=rep, dh=dh,
                             idx_h=idx_h, idx_d=idx_d, chunk=CHUNK, idx_bits=max(1, (T - 1).bit_length()))
    return pl.pallas_call(
        kern,
        grid=(B, T // tq),
        in_specs=[pl.BlockSpec((1, tq, qd), lambda b, i: (b, i, 0)),
                  pl.BlockSpec((1, tq, idd), lambda b, i: (b, i, qd // idd)),
                  pl.BlockSpec((1, tq, idx_h), lambda b, i: (b, i, 0)),
                  pl.BlockSpec((1, T, idx_d), lambda b, i: (b, 0, 0)),
                  pl.BlockSpec((1, T, n_kv * dh), lambda b, i: (b, 0, 0)),
                  pl.BlockSpec((1, T, n_kv * dh), lambda b, i: (b, 0, 0)),
                  pl.BlockSpec((n_kv, rep * tq, 2 * tq), lambda b, i: (0, 0, 0))],
        out_specs=pl.BlockSpec((1, tq, qd), lambda b, i: (b, i, 0)),
        out_shape=jax.ShapeDtypeStruct((B, T, qd), BF16),
        scratch_shapes=[pltpu.VMEM((tq, T), I32), pltpu.VMEM((tq, LANES), I32),
                        pltpu.VMEM((n_kv, rep * tq, LANES), F32), pltpu.VMEM((n_kv, rep * tq, 2 * dh), F32)],
        compiler_params=_cparams("parallel", "arbitrary"),
        name="dsa_prompt",
    )(qa, qa, wi, ki, kb, vb, bias)


def _dsa_sample_kernel(q_ref, qi_ref, wi_ref, ki_ref, k_ref, v_ref, bias_ref, o_ref, keys_ref,
                       *, tq, S, n_valid, topk, n_kv, rep, dh, idx_h, idx_d, idx_bits):
    wi = wi_ref[0]
    qi = qi_ref[0]
    kt = ki_ref[0]
    sc = jnp.zeros((tq, S), F32)
    for h in range(idx_h):
        sc = sc + wi[:, h:h + 1] * jnp.maximum(_dot_nt(qi[:, h * idx_d:(h + 1) * idx_d], kt), 0.0)
    sc = sc * (idx_d ** -0.5)
    col = lax.broadcasted_iota(I32, (tq, S), 1)
    valid = col < n_valid
    keys_ref[...] = _sortable_key(jnp.where(valid, sc, -jnp.inf))
    thr = _topk_threshold(keys_ref, S // LANES, LANES, tq, topk, idx_bits)[:, :1]
    madd = jnp.where(keys_ref[...] >= thr, jnp.where(valid, 0.0, NEG), NEG)
    madd = jnp.concatenate([madd] * rep, axis=0)
    q = q_ref[0]
    for g in range(n_kv):
        qg = jnp.concatenate([q[:, (g * rep + r) * dh:(g * rep + r + 1) * dh] for r in range(rep)], axis=0)
        s = _dot_nt(qg, k_ref[0, :, g * dh:(g + 1) * dh]) + bias_ref[g] + madd
        p = jnp.exp2(s - jnp.max(s, axis=1, keepdims=True))
        out = _dot(p.astype(BF16), v_ref[0, :, g * dh:(g + 1) * dh]) / jnp.sum(p, axis=1, keepdims=True)
        for r in range(rep):
            hcol = (g * rep + r) * dh
            o_ref[0, :, hcol:hcol + dh] = out[r * tq:(r + 1) * tq].astype(o_ref.dtype)


def _dsa_sample(qa, wi, ki_all, k_all, v_all, rel_bias, *, past, n_heads, n_kv, dh, idx_h, idx_d):
    B, tq, _ = qa.shape
    S = k_all.shape[1]
    n_valid = past + tq
    topk = min(TOPK_MAX, n_valid // 4)
    rep = n_heads // n_kv
    qd = n_heads * dh
    idd = idx_h * idx_d
    rel = np.minimum(np.arange(S), n_valid - 1)[None, :] - (past + np.arange(tq))[:, None]
    bias = _bias_lookup(_t5_bucket_np(rel), rel_bias) * LOG2E
    bias = jnp.transpose(bias.reshape(tq, S, n_kv, rep), (2, 3, 0, 1)).reshape(n_kv, rep * tq, S)
    kern = functools.partial(_dsa_sample_kernel, tq=tq, S=S, n_valid=n_valid, topk=topk, n_kv=n_kv, rep=rep,
                             dh=dh, idx_h=idx_h, idx_d=idx_d, idx_bits=max(1, (S - 1).bit_length()))
    return pl.pallas_call(
        kern,
        grid=(B,),
        in_specs=[pl.BlockSpec((1, tq, qd), lambda b: (b, 0, 0)),
                  pl.BlockSpec((1, tq, idd), lambda b: (b, 0, qd // idd)),
                  pl.BlockSpec((1, tq, idx_h), lambda b: (b, 0, 0)),
                  pl.BlockSpec((1, S, idx_d), lambda b: (b, 0, 0)),
                  pl.BlockSpec((1, S, n_kv * dh), lambda b: (b, 0, 0)),
                  pl.BlockSpec((1, S, n_kv * dh), lambda b: (b, 0, 0)),
                  pl.BlockSpec((n_kv, rep * tq, S), lambda b: (0, 0, 0))],
        out_specs=pl.BlockSpec((1, tq, qd), lambda b: (b, 0, 0)),
        out_shape=jax.ShapeDtypeStruct((B, tq, qd), BF16),
        scratch_shapes=[pltpu.VMEM((tq, S), I32)],
        compiler_params=_cparams("parallel"),
        name="dsa_sample",
    )(qa, qa, wi, ki_all, k_all, v_all, bias)


def _dsa_tail_epilogue(accs, extras, j, *, idx_d, idx_h):
    a = accs[0]
    ki = _rms(a[:, :idx_d], extras[0][:, :idx_d])
    return [jnp.concatenate([ki, a[:, idx_d:] * idx_h ** -0.5], axis=1)]


def _dsa_layer(h3, cache, gain, w_qa, w_kv, w_tail, g_q, g_k, g_ki, w_out, rel_bias,
               *, n_heads, n_kv, dh, idx_h, idx_d):
    B, T, D = h3.shape
    M = B * T
    qd = n_heads * dh
    kvd = n_kv * dh
    h2 = h3.reshape(M, D)
    tn = 512
    g_qa = jnp.concatenate([jnp.tile(g_q, n_heads), jnp.ones((idx_h * idx_d,), F32)]).reshape(1, -1)
    qa = _mm(h2, [w_qa], functools.partial(_ep_headnorm, dh=dh, n_norm_tiles=qd // tn,
                                           post_scale=dh ** -0.5 * LOG2E), [BF16],
             gain=gain, extras=[g_qa], tn=tn, name="dsa_q_proj")[0].reshape(B, T, -1)
    g_kv = jnp.concatenate([jnp.tile(g_k, n_kv), jnp.ones((kvd,), F32)]).reshape(1, -1)
    kv = _mm(h2, [w_kv], functools.partial(_ep_headnorm, dh=dh, n_norm_tiles=kvd // tn), [F32],
             gain=gain, extras=[g_kv], tn=tn, name="dsa_kv_proj")[0]
    g_tail = jnp.concatenate([g_ki, jnp.ones((idx_h,), F32)]).reshape(1, -1)
    tail = _mm(h2, [w_tail], functools.partial(_dsa_tail_epilogue, idx_d=idx_d, idx_h=idx_h), [F32],
               gain=gain, extras=[g_tail], name="dsa_idx_proj")[0]
    k_new = kv[:, :kvd].reshape(B, T, kvd)
    v_new = kv[:, kvd:].reshape(B, T, kvd)
    ki_new = tail[:, :idx_d].reshape(B, T, idx_d)
    wi = tail[:, idx_d:].reshape(B, T, idx_h)
    dims = dict(n_heads=n_heads, n_kv=n_kv, dh=dh, idx_h=idx_h, idx_d=idx_d)
    if cache is None:
        att = _dsa_prompt(qa, wi, ki_new.astype(BF16), k_new.astype(BF16), v_new.astype(BF16), rel_bias, **dims)
    else:
        ck, cv, cki = cache
        past = ck.shape[1]
        S = -(-(past + T) // LANES) * LANES
        padr = ((0, 0), (0, S - past - T), (0, 0))
        k_all = jnp.pad(jnp.concatenate([ck.reshape(B, past, kvd), k_new], axis=1).astype(BF16), padr)
        v_all = jnp.pad(jnp.concatenate([cv.reshape(B, past, kvd), v_new], axis=1).astype(BF16), padr)
        ki_all = jnp.pad(jnp.concatenate([cki, ki_new], axis=1).astype(BF16), padr)
        att = _dsa_sample(qa, wi, ki_all, k_all, v_all, rel_bias, past=past, **dims)
    h_new = _mm(att.reshape(M, qd), [w_out], _ep_residual, [F32], extras=[h2], name="dsa_out_proj")[0]
    return (h_new.reshape(B, T, D), k_new.reshape(B, T, n_kv, dh), v_new.reshape(B, T, n_kv, dh), ki_new)


def _router_kernel(h_ref, g_ref, wr_ref, xn_ref, ro_ref, *, n_exp):
    x = _rms(h_ref[...], g_ref[...])
    xn_ref[...] = x.astype(BF16)
    logits = _dot_hi(x, wr_ref[...])
    lane = lax.broadcasted_iota(I32, logits.shape, 1).astype(F32)
    lg = jnp.where(lane < n_exp, logits, -jnp.inf)
    m1 = jnp.max(lg, axis=1, keepdims=True)
    i1 = jnp.min(jnp.where(lg == m1, lane, float(LANES)), axis=1, keepdims=True)
    lg2 = jnp.where(lane == i1, -jnp.inf, lg)
    m2 = jnp.max(lg2, axis=1, keepdims=True)
    i2 = jnp.min(jnp.where(lg2 == m2, lane, float(LANES)), axis=1, keepdims=True)
    e = jnp.exp(m2 - m1)
    g1 = 1.0 / (1.0 + e)
    g2 = e / (1.0 + e)
    ro_ref[...] = jnp.where(lane == 0, i1, jnp.where(lane == 1, i2, jnp.where(lane == 2, g1,
                            jnp.where(lane == 3, g2, 0.0))))


def _router(h2, gain, w_router, *, tm=512):
    M, D = h2.shape
    n_exp = w_router.shape[1]
    tm = _pick_tile(M, tm)
    wr = jnp.pad(w_router.astype(F32), ((0, 0), (0, LANES - n_exp)))
    return pl.pallas_call(
        functools.partial(_router_kernel, n_exp=n_exp),
        grid=(M // tm,),
        in_specs=[pl.BlockSpec((tm, D), lambda i: (i, 0)), pl.BlockSpec((1, D), lambda i: (0, 0)),
                  pl.BlockSpec((D, LANES), lambda i: (0, 0))],
        out_specs=[pl.BlockSpec((tm, D), lambda i: (i, 0)), pl.BlockSpec((tm, LANES), lambda i: (i, 0))],
        out_shape=[jax.ShapeDtypeStruct((M, D), BF16), jax.ShapeDtypeStruct((M, LANES), F32)],
        compiler_params=_cparams("parallel"),
        name="moe_router",
    )(h2, gain.reshape(1, D).astype(F32), wr)


def _moe(h_list, gain, w_router, wg, wu, wd, *, bm=512):
    n_exp = w_router.shape[1]
    routed = [_router(h, gain, w_router) for h in h_list]
    xn = jnp.concatenate([r[0] for r in routed], axis=0)
    ro = jnp.concatenate([r[1] for r in routed], axis=0)
    N, D = xn.shape
    A = N * MOE_TOP_K
    flat_e = ro[:, :MOE_TOP_K].astype(I32).reshape(-1)
    onehot = (flat_e[:, None] == jnp.arange(n_exp, dtype=I32)[None, :]).astype(I32)
    csum = jnp.cumsum(onehot, axis=0)
    rank = jnp.take_along_axis(csum, flat_e[:, None], axis=1)[:, 0] - 1
    counts = csum[-1]
    padded = (counts + bm - 1) // bm * bm
    pad_end = jnp.cumsum(padded)
    dest = (pad_end - padded)[flat_e] + rank
    n_blocks = (A + n_exp * (bm - 1)) // bm
    row_src = jnp.zeros((n_blocks * bm,), I32).at[dest].set(jnp.arange(A, dtype=I32) // MOE_TOP_K)
    block_e = jnp.minimum(jnp.searchsorted(pad_end, jnp.arange(n_blocks, dtype=I32) * bm, side='right'),
                          n_exp - 1).astype(I32)
    n_used = (pad_end[-1:] // bm).astype(I32)
    xs = xn.at[row_src].get(mode="promise_in_bounds")
    yp = _ffn_experts(xs, block_e, n_used, wg, wu, wd, bm=bm)
    dest2 = dest.reshape(N, MOE_TOP_K)
    y0 = yp.at[dest2[:, 0]].get(mode="promise_in_bounds")
    y1 = yp.at[dest2[:, 1]].get(mode="promise_in_bounds")
    return y0, y1, ro


def kernel(x_prompt, x_sample, state_gdn, state_gdn_conv, cache_dsa_k, cache_dsa_v, cache_dsa_kidx, p_prompt, p_sample, g_mix, g_ffn, g_ple, w_ple_in, w_ple_gate, w_in_gdn, conv_w_gdn, a_log_gdn, dt_bias_gdn, g_onorm_gdn, w_out_gdn, w_in_dsa, g_qnorm_dsa, g_knorm_dsa, g_kidx_norm_dsa, w_out_dsa, rel_bias, w_gate_ffn, w_up_ffn, w_down_ffn, w_router, w_gate_moe, w_up_moe, w_down_moe):
    depth = g_mix.shape[0]
    Bp, Tp, D = x_prompt.shape
    Bs, Ts, _ = x_sample.shape
    nv = a_log_gdn.shape[1]
    dk, dv = state_gdn.shape[3], state_gdn.shape[4]
    conv_dim = conv_w_gdn.shape[2]
    nk = (conv_dim - nv * dv) // (2 * dk)
    gdn_dims = dict(nk=nk, nv=nv, dk=dk, dv=dv)
    n_heads = rel_bias.shape[1]
    n_kv, dh = cache_dsa_k.shape[3], cache_dsa_k.shape[4]
    idx_d = cache_dsa_kidx.shape[3]
    qd, kvd = n_heads * dh, n_kv * dh
    idx_h = (w_in_dsa.shape[2] - qd - 2 * kvd - idx_d) // (idx_d + 1)
    dsa_dims = dict(n_heads=n_heads, n_kv=n_kv, dh=dh, idx_h=idx_h, idx_d=idx_d)

    hp, hs = x_prompt, x_sample
    outs = {k: [] for k in ("sp", "cp", "ss", "cs", "kp", "vp", "kip", "ks", "vs", "kis")}
    for i in range(depth):
        j = i // 2
        if i % 2 == 0:
            w_in = w_in_gdn[j]
            n_main = conv_dim + nv * dv
            wa = (g_mix[i], w_in[:, :n_main].astype(BF16), w_in[:, n_main:].astype(BF16), conv_w_gdn[j],
                  a_log_gdn[j], dt_bias_gdn[j], g_onorm_gdn[j], w_out_gdn[j].astype(BF16))
            hp, cbp, sp = _gdn_layer(hp, None, None, *wa, **gdn_dims)
            hs, cbs, ss = _gdn_layer(hs, state_gdn_conv[j], state_gdn[j], *wa, **gdn_dims)
            outs["sp"].append(sp); outs["cp"].append(cbp); outs["ss"].append(ss); outs["cs"].append(cbs)
            wf = (g_ffn[i], w_gate_ffn[j].astype(BF16), w_up_ffn[j].astype(BF16), w_down_ffn[j].astype(BF16))
            hp = _ffn_dense(hp.reshape(Bp * Tp, D), *wf).reshape(Bp, Tp, D)
            hs = _ffn_dense(hs.reshape(Bs * Ts, D), *wf).reshape(Bs, Ts, D)
            moe_p = moe_s = None
        else:
            w_in = w_in_dsa[j]
            o1 = qd + 2 * kvd
            o2 = o1 + idx_h * idx_d
            w_qa = jnp.concatenate([w_in[:, :qd], w_in[:, o1:o2]], axis=1).astype(BF16)
            wb = (g_mix[i], w_qa, w_in[:, qd:o1].astype(BF16), w_in[:, o2:].astype(BF16), g_qnorm_dsa[j],
                  g_knorm_dsa[j], g_kidx_norm_dsa[j], w_out_dsa[j].astype(BF16), rel_bias)
            hp, k, v, ki = _dsa_layer(hp, None, *wb, **dsa_dims)
            outs["kp"].append(k); outs["vp"].append(v); outs["kip"].append(ki)
            hs, k, v, ki = _dsa_layer(hs, (cache_dsa_k[j], cache_dsa_v[j], cache_dsa_kidx[j]), *wb, **dsa_dims)
            outs["ks"].append(k); outs["vs"].append(v); outs["kis"].append(ki)
            y0, y1, ro = _moe([hp.reshape(Bp * Tp, D), hs.reshape(Bs * Ts, D)], g_ffn[i], w_router[j],
                              w_gate_moe[j].astype(BF16), w_up_moe[j].astype(BF16), w_down_moe[j].astype(BF16))
            moe_p, moe_s = (y0, y1, ro, 0), (y0, y1, ro, Bp * Tp)
        wpl = (g_ple[i], w_ple_in[i].astype(BF16), w_ple_gate[i].astype(BF16))
        hp = _ple_add(hp.reshape(Bp * Tp, D), p_prompt[i].reshape(Bp * Tp, -1), *wpl, moe=moe_p).reshape(Bp, Tp, D)
        hs = _ple_add(hs.reshape(Bs * Ts, D), p_sample[i].reshape(Bs * Ts, -1), *wpl, moe=moe_s).reshape(Bs, Ts, D)
    st = lambda name: jnp.stack(outs[name])
    return (hp, hs, st("sp"), st("cp"), st("kp"), st("vp"), st("kip"),
            st("ss"), st("cs"), st("ks"), st("vs"), st("kis"))
```

```python
import functools
import math

import numpy as np
import jax
import jax.numpy as jnp
from jax import lax
from jax.experimental import pallas as pl
from jax.experimental.pallas import tpu as pltpu

F32 = jnp.float32
BF16 = jnp.bfloat16
I32 = jnp.int32

EPS = 1e-6
CHUNK = 64
TOPK_MAX = 256
MOE_TOP_K = 2
NUM_BUCKETS = 32
MAX_DISTANCE = 128
NEG = -1e30
INT_MIN = -(2 ** 31)
LOG2E = math.log2(math.e)
HI = lax.Precision.HIGHEST

V7X_VMEM_LIMIT = 56 * 1024 * 1024
LANES = 128


def _cparams(*sem):
    return pltpu.CompilerParams(dimension_semantics=sem, vmem_limit_bytes=V7X_VMEM_LIMIT)


def _sigmoid(x):
    return 1.0 / (1.0 + jnp.exp(-x))


def _silu(x):
    return x * _sigmoid(x)


def _rms(x, g):
    return x * lax.rsqrt(jnp.mean(x * x, axis=-1, keepdims=True) + EPS) * g


def _dot(a, b):
    return jnp.dot(a, b, preferred_element_type=F32)


def _dot_nt(a, b):
    return lax.dot_general(a, b, (((1,), (1,)), ((), ())), preferred_element_type=F32)


def _dot_tn(a, b):
    return lax.dot_general(a, b, (((0,), (0,)), ((), ())), preferred_element_type=F32)


def _dot_hi(a, b):
    return jnp.dot(a, b, preferred_element_type=F32, precision=HI)


def _pick_tile(n, pref):
    t = min(n, pref)
    while n % t:
        t //= 2
    return t


def _mm_kernel(*refs, has_gain, n_w, n_extra, n_out, epilogue):
    x_ref = refs[0]
    pos = 1
    g_ref = None
    if has_gain:
        g_ref = refs[pos]
        pos += 1
    w_refs = refs[pos:pos + n_w]
    pos += n_w
    e_refs = refs[pos:pos + n_extra]
    pos += n_extra
    o_refs = refs[pos:pos + n_out]
    pos += n_out
    xs_ref = refs[pos]

    @pl.when(pl.program_id(1) == 0)
    def _():
        x = x_ref[...].astype(F32)
        if has_gain:
            x = _rms(x, g_ref[...])
        xs_ref[...] = x.astype(BF16)

    xs = xs_ref[...]
    accs = [_dot(xs, w[...]) for w in w_refs]
    outs = epilogue(accs, [e[...] for e in e_refs], pl.program_id(1))
    for o_ref, o in zip(o_refs, outs):
        o_ref[...] = o.astype(o_ref.dtype)


def _mm(x, ws, epilogue, out_dtypes, *, gain=None, extras=(), tm=1024, tn=512, name=None):
    M, K = x.shape
    N = ws[0].shape[1]
    tm = _pick_tile(M, tm)
    tn = _pick_tile(N, tn)
    in_specs = [pl.BlockSpec((tm, K), lambda i, j: (i, 0))]
    args = [x]
    if gain is not None:
        in_specs.append(pl.BlockSpec((1, K), lambda i, j: (0, 0)))
        args.append(gain.reshape(1, K).astype(F32))
    for w in ws:
        in_specs.append(pl.BlockSpec((K, tn), lambda i, j: (0, j)))
        args.append(w)
    for e in extras:
        if e.shape[0] == 1:
            in_specs.append(pl.BlockSpec((1, tn), lambda i, j: (0, j)))
        else:
            in_specs.append(pl.BlockSpec((tm, tn), lambda i, j: (i, j)))
        args.append(e)
    out_shape = [jax.ShapeDtypeStruct((M, N), d) for d in out_dtypes]
    out_specs = [pl.BlockSpec((tm, tn), lambda i, j: (i, j)) for _ in out_dtypes]
    kern = functools.partial(_mm_kernel, has_gain=gain is not None, n_w=len(ws), n_extra=len(extras),
                             n_out=len(out_dtypes), epilogue=epilogue)
    return pl.pallas_call(
        kern,
        grid=(M // tm, N // tn),
        in_specs=in_specs,
        out_specs=out_specs,
        out_shape=out_shape,
        scratch_shapes=[pltpu.VMEM((tm, K), BF16)],
        compiler_params=_cparams("parallel", "arbitrary"),
        name=name,
    )(*args)


def _ep_plain(accs, extras, j):
    return [accs[0]]


def _ep_residual(accs, extras, j):
    return [extras[0] + accs[0]]


def _ep_headnorm(accs, extras, j, *, dh, n_norm_tiles, post_scale=1.0):
    a = accs[0]
    g = extras[0]
    cols = []
    for h in range(a.shape[1] // dh):
        sl = slice(h * dh, (h + 1) * dh)
        cols.append(_rms(a[:, sl], g[:, sl]) * post_scale)
    normed = jnp.concatenate(cols, axis=1) if len(cols) > 1 else cols[0]
    return [jnp.where(j < n_norm_tiles, normed, a)]


def _ffn_kernel(*refs, has_gain, has_be):
    if has_be:
        used = pl.program_id(0) < refs[1][0]
        pl.when(used)(lambda: _ffn_body(refs[2:], has_gain))
        o_ref = refs[6]

        @pl.when(jnp.logical_not(used))
        def _():
            o_ref[...] = jnp.zeros_like(o_ref)
    else:
        _ffn_body(refs, has_gain)


def _ffn_body(refs, has_gain):
    pos = 0
    x_ref = refs[pos]
    pos += 1
    g_ref = None
    if has_gain:
        g_ref = refs[pos]
        pos += 1
    wg_ref, wu_ref, wd_ref, o_ref, scr_ref = refs[pos:pos + 5]
    j = pl.program_id(1)
    if has_gain:
        xs_ref, acc_ref = scr_ref, o_ref
    else:
        xs_ref, acc_ref = x_ref, scr_ref

    @pl.when(j == 0)
    def _():
        if has_gain:
            x = x_ref[...].astype(F32)
            xs_ref[...] = _rms(x, g_ref[...]).astype(BF16)
            acc_ref[...] = x
        else:
            acc_ref[...] = jnp.zeros_like(acc_ref)

    xs = xs_ref[...]
    wg = wg_ref[...].reshape(wg_ref.shape[-2:])
    wu = wu_ref[...].reshape(wu_ref.shape[-2:])
    wd = wd_ref[...].reshape(wd_ref.shape[-2:])
    a = (_silu(_dot(xs, wg)) * _dot(xs, wu)).astype(BF16)
    acc_ref[...] += _dot(a, wd)

    if not has_gain:
        @pl.when(j == pl.num_programs(1) - 1)
        def _():
            o_ref[...] = acc_ref[...].astype(o_ref.dtype)


def _ffn_dense(h, gain, wg, wu, wd, *, tm=1024, tf=512):
    M, D = h.shape
    Fdim = wg.shape[1]
    tm = _pick_tile(M, tm)
    tf = _pick_tile(Fdim, tf)
    return pl.pallas_call(
        functools.partial(_ffn_kernel, has_gain=True, has_be=False),
        grid=(M // tm, Fdim // tf),
        in_specs=[pl.BlockSpec((tm, D), lambda i, j: (i, 0)),
                  pl.BlockSpec((1, D), lambda i, j: (0, 0)),
                  pl.BlockSpec((D, tf), lambda i, j: (0, j)),
                  pl.BlockSpec((D, tf), lambda i, j: (0, j)),
                  pl.BlockSpec((tf, D), lambda i, j: (j, 0))],
        out_specs=pl.BlockSpec((tm, D), lambda i, j: (i, 0)),
        out_shape=jax.ShapeDtypeStruct((M, D), F32),
        scratch_shapes=[pltpu.VMEM((tm, D), BF16)],
        compiler_params=_cparams("parallel", "arbitrary"),
        name="ffn_dense",
    )(h, gain.reshape(1, D).astype(F32), wg, wu, wd)


def _ffn_experts(xs, block_e, n_used, wg, wu, wd, *, bm, tf=512):
    P, D = xs.shape
    Fdim = wg.shape[2]
    tf = _pick_tile(Fdim, tf)
    nf = Fdim // tf

    def row(i, nu):
        return jnp.minimum(i, nu[0] - 1)

    def col(i, j, nu):
        return jnp.where(i < nu[0], j, nf - 1)

    grid_spec = pltpu.PrefetchScalarGridSpec(
        num_scalar_prefetch=2,
        grid=(P // bm, nf),
        in_specs=[pl.BlockSpec((bm, D), lambda i, j, be, nu: (row(i, nu), 0)),
                  pl.BlockSpec((1, D, tf), lambda i, j, be, nu: (be[row(i, nu)], 0, col(i, j, nu))),
                  pl.BlockSpec((1, D, tf), lambda i, j, be, nu: (be[row(i, nu)], 0, col(i, j, nu))),
                  pl.BlockSpec((1, tf, D), lambda i, j, be, nu: (be[row(i, nu)], col(i, j, nu), 0))],
        out_specs=pl.BlockSpec((bm, D), lambda i, j, be, nu: (i, 0)),
        scratch_shapes=[pltpu.VMEM((bm, D), F32)],
    )
    return pl.pallas_call(
        functools.partial(_ffn_kernel, has_gain=False, has_be=True),
        grid_spec=grid_spec,
        out_shape=jax.ShapeDtypeStruct((P, D), BF16),
        compiler_params=_cparams("parallel", "arbitrary"),
        name="moe_experts",
    )(block_e, n_used, xs, wg, wu, wd)


def _ple_kernel(*refs, has_moe, tn):
    if has_moe:
        h_ref, y0_ref, y1_ref, ro_ref, g_ref, wgate_ref, p_ref, wp_ref, o_ref, xs_ref, hs_ref = refs
    else:
        h_ref, g_ref, wgate_ref, p_ref, wp_ref, o_ref, xs_ref, hs_ref = refs
    j = pl.program_id(1)

    @pl.when(j == 0)
    def _():
        h = h_ref[...]
        if has_moe:
            ro = ro_ref[...]
            h = h + (ro[:, MOE_TOP_K:MOE_TOP_K + 1] * y0_ref[...].astype(F32)
                     + ro[:, MOE_TOP_K + 1:MOE_TOP_K + 2] * y1_ref[...].astype(F32))
        hs_ref[...] = h
        xs_ref[...] = _rms(h, g_ref[...]).astype(BF16)

    gate = _sigmoid(_dot(xs_ref[...], wgate_ref[...]))
    ht = hs_ref[:, pl.ds(pl.multiple_of(j * tn, tn), tn)]
    o_ref[...] = ht + gate * _dot(p_ref[...].astype(BF16), wp_ref[...])


def _ple_add(h, p, gain, wp, wgate, *, moe=None, tm=1024, tn=512):
    M, D = h.shape
    Pd = p.shape[1]
    tm = _pick_tile(M, tm if moe is None else 512)
    tn = _pick_tile(D, tn)
    in_specs = [pl.BlockSpec((tm, D), lambda i, j: (i, 0))]
    args = [h]
    if moe is not None:
        y0, y1, ro, first_row = moe
        assert first_row % tm == 0
        off = first_row // tm
        in_specs += [pl.BlockSpec((tm, D), lambda i, j: (i + off, 0)),
                     pl.BlockSpec((tm, D), lambda i, j: (i + off, 0)),
                     pl.BlockSpec((tm, LANES), lambda i, j: (i + off, 0))]
        args += [y0, y1, ro]
    in_specs += [pl.BlockSpec((1, D), lambda i, j: (0, 0)),
                 pl.BlockSpec((D, tn), lambda i, j: (0, j)),
                 pl.BlockSpec((tm, Pd), lambda i, j: (i, 0)),
                 pl.BlockSpec((Pd, tn), lambda i, j: (0, j))]
    args += [gain.reshape(1, D).astype(F32), wgate, p, wp]
    return pl.pallas_call(
        functools.partial(_ple_kernel, has_moe=moe is not None, tn=tn),
        grid=(M // tm, D // tn),
        in_specs=in_specs,
        out_specs=pl.BlockSpec((tm, tn), lambda i, j: (i, j)),
        out_shape=jax.ShapeDtypeStruct((M, D), F32),
        scratch_shapes=[pltpu.VMEM((tm, D), BF16), pltpu.VMEM((tm, D), F32)],
        compiler_params=_cparams("parallel", "arbitrary"),
        name="ple_add",
    )(*args)


def _gdn_gate_epilogue(accs, extras, j, *, nv):
    ba = accs[0]
    a_log, dt = extras
    beta = _sigmoid(ba[:, :nv])
    x = ba[:, nv:] + dt[:, nv:]
    softplus = jnp.maximum(x, 0.0) + jnp.log(1.0 + jnp.exp(-jnp.abs(x)))
    g = -jnp.exp(a_log[:, nv:]) * softplus
    return [jnp.concatenate([beta, g], axis=1)]


def _gdn_conv_kernel(x_ref, buf_ref, w_ref, y_ref, cs_ref, xe_ref, *, tt, tc, dk, n_q_blocks, n_qk_blocks):
    cb = pl.program_id(1)
    t = pl.program_id(2)

    @pl.when(t == 0)
    def _():
        xe_ref[0:8, :] = buf_ref[0]

    @pl.when(t > 0)
    def _():
        xe_ref[0:8, :] = xe_ref[tt:tt + 8, :]

    x = x_ref[0]
    xf = x.astype(F32)
    xe_ref[8:8 + tt, :] = xf
    w = w_ref[...]
    rows = lax.broadcasted_iota(I32, (tt, tt), 0)
    cols = lax.broadcasted_iota(I32, (tt, tt), 1)
    acc = w[3:4] * xf
    for j in range(1, 4):
        shift = jnp.where(rows - cols == j, 1.0, 0.0).astype(BF16)
        acc = acc + w[3 - j:4 - j] * _dot(shift, x)
    head = (w[3:4] * xe_ref[8:16, :] + w[2:3] * xe_ref[7:15, :]
            + w[1:2] * xe_ref[6:14, :] + w[0:1] * xe_ref[5:13, :])
    y = _silu(jnp.concatenate([head, acc[8:]], axis=0))
    cs_ref[0] = xe_ref[tt:tt + 8, :]

    @pl.when(cb < n_qk_blocks)
    def _():
        scale = jnp.where(cb < n_q_blocks, dk ** -0.5, 1.0).astype(F32)
        for h in range(tc // dk):
            yh = y[:, h * dk:(h + 1) * dk]
            y_ref[0, :, h * dk:(h + 1) * dk] = (
                yh * lax.rsqrt(jnp.sum(yh * yh, axis=-1, keepdims=True) + EPS) * scale).astype(y_ref.dtype)

    @pl.when(cb >= n_qk_blocks)
    def _():
        y_ref[0] = y.astype(y_ref.dtype)


def _gdn_conv(qkvz, buf8, conv_w, *, conv_dim, qk_dim, dk, tt=256, tc=1024):
    B, T, _ = qkvz.shape
    tt = _pick_tile(T, tt)
    assert tt >= 8 and conv_dim % tc == 0 and qk_dim % tc == 0 and qkvz.dtype == BF16
    kern = functools.partial(_gdn_conv_kernel, tt=tt, tc=tc, dk=dk, n_q_blocks=qk_dim // tc,
                             n_qk_blocks=2 * qk_dim // tc)
    return pl.pallas_call(
        kern,
        grid=(B, conv_dim // tc, T // tt),
        in_specs=[pl.BlockSpec((1, tt, tc), lambda b, c, t: (b, t, c)),
                  pl.BlockSpec((1, 8, tc), lambda b, c, t: (b, 0, c)),
                  pl.BlockSpec((4, tc), lambda b, c, t: (0, c))],
        out_specs=[pl.BlockSpec((1, tt, tc), lambda b, c, t: (b, t, c)),
                   pl.BlockSpec((1, 8, tc), lambda b, c, t: (b, 0, c))],
        out_shape=[jax.ShapeDtypeStruct((B, T, conv_dim), BF16),
                   jax.ShapeDtypeStruct((B, 8, conv_dim), F32)],
        scratch_shapes=[pltpu.VMEM((tt + 8, tc), F32)],
        compiler_params=_cparams("parallel", "parallel", "arbitrary"),
        name="gdn_conv",
    )(qkvz, buf8, conv_w)


def _dot_b(a, b):
    return _dot(a.astype(BF16), b.astype(BF16))


def _gdn_scan_kernel(q_ref, k_ref, v_ref, z_ref, gc_ref, gr_ref, *rest, C, cps, Gv, dk, dv, rep, has_s0):
    if has_s0:
        s0_ref, gon_ref, o_ref, sN_ref, S_scr = rest
    else:
        gon_ref, o_ref, sN_ref, S_scr = rest
    t = pl.program_id(2)

    @pl.when(t == 0)
    def _():
        if has_s0:
            S_scr[...] = s0_ref[0]
        else:
            S_scr[...] = jnp.zeros_like(S_scr)

    rows = lax.broadcasted_iota(I32, (C, C), 0)
    cols = lax.broadcasted_iota(I32, (C, C), 1)
    incl = rows >= cols
    strict = rows > cols
    ltri = jnp.where(incl, 1.0, 0.0).astype(F32)
    eye = jnp.where(rows == cols, 1.0, 0.0).astype(F32)
    gon = gon_ref[...]

    chains = [(c, hh) for c in range(cps) for hh in range(Gv)]
    gcm, gcum_col, gcum_row = [], [], []
    for c in range(cps):
        r0 = c * C
        m = gc_ref[0, 0, r0:r0 + C, :]
        gcm.append(m)
        gcum_col.append(_dot_hi(ltri, m[:, Gv:]))
        gcum_row.append(lax.dot_general(gr_ref[0, 0, Gv:, r0:r0 + C], ltri, (((1,), (1,)), ((), ())),
                                        preferred_element_type=F32, precision=HI))

    st = []
    for (c, hh) in chains:
        r0 = c * C
        kh = hh // rep
        qh = q_ref[0, r0:r0 + C, kh * dk:(kh + 1) * dk].astype(F32)
        kk = k_ref[0, r0:r0 + C, kh * dk:(kh + 1) * dk].astype(F32)
        vh = v_ref[0, r0:r0 + C, hh * dv:(hh + 1) * dv].astype(F32)
        beta = gcm[c][:, hh:hh + 1]
        gcol = gcum_col[c][:, hh:hh + 1]
        grow = gcum_row[c][hh:hh + 1, :]
        glast = grow[:, C - 1:C]
        decay = jnp.exp(jnp.where(incl, gcol - grow, NEG))
        eg = jnp.exp(gcol)
        kb = kk * beta
        kq = _dot_nt(jnp.concatenate([kb, qh], axis=0).astype(BF16), kk.astype(BF16))
        p = jnp.where(strict, -(kq[:C] * decay), 0.0)
        st.append(dict(
            p=p, y=eye + p, qk=(kq[C:] * decay).astype(BF16),
            rhs=jnp.concatenate([vh * beta, kb * eg], axis=1),
            qe=qh * eg, kdT=jnp.transpose(kk * jnp.exp(glast - gcol)).astype(BF16),
            egl=jnp.exp(glast)))

    for s in st:
        s["p"] = _dot_b(s["p"], s["p"])
    m = 4
    while m < C:
        for s in st:
            r = _dot_b(jnp.concatenate([s["p"], s["y"]], axis=0), s["p"])
            s["p"] = r[:C]
            s["y"] = s["y"] + r[C:]
        m *= 2
    for s in st:
        s["y"] = s["y"] + _dot_b(s["y"], s["p"])
    for s in st:
        sol = _dot_b(s["y"], s["rhs"])
        s["u"] = sol[:, :dv]
        s["wq"] = jnp.concatenate([sol[:, dv:], s["qe"]], axis=0).astype(BF16)
        s["lhs2"] = jnp.concatenate([s["qk"], s["kdT"]], axis=0)

    S = [S_scr[hh] for hh in range(Gv)]
    for c in range(cps):
        r0 = c * C
        cur = [st[c * Gv + hh] for hh in range(Gv)]
        ws = [_dot(s["wq"], S[hh].astype(BF16)) for hh, s in enumerate(cur)]
        v_new = [(s["u"] - ws[hh][:C]).astype(BF16) for hh, s in enumerate(cur)]
        r2 = [_dot(s["lhs2"], v_new[hh]) for hh, s in enumerate(cur)]
        for hh, s in enumerate(cur):
            o = ws[hh][C:] + r2[hh][:C]
            S[hh] = S[hh] * s["egl"] + r2[hh][C:]
            zz = z_ref[0, r0:r0 + C, hh * dv:(hh + 1) * dv].astype(F32)
            o_ref[0, r0:r0 + C, hh * dv:(hh + 1) * dv] = (_rms(o, gon) * _silu(zz)).astype(o_ref.dtype)
    for hh in range(Gv):
        S_scr[hh] = S[hh]

    @pl.when(t == pl.num_programs(2) - 1)
    def _():
        sN_ref[0] = S_scr[...]


def _gdn_scan(y, qkvz, gates, s0, g_onorm, *, nk, nv, dk, dv, Gv=8):
    B, T, conv_dim = y.shape
    rep = nv // nk
    C = min(CHUNK, T)
    cps = 1 if C == T else max(1, 2 * LANES // C)
    TB = C * cps
    assert T % TB == 0 and Gv % rep == 0 and nv % Gv == 0
    Gk = Gv // rep
    HB = nv // Gv
    beta = gates[..., :nv].reshape(B, T, HB, Gv)
    g = gates[..., nv:].reshape(B, T, HB, Gv)
    gcol = jnp.transpose(jnp.concatenate([beta, g], axis=-1), (0, 2, 1, 3))
    grow = jnp.transpose(gcol, (0, 1, 3, 2))
    k_off = (nk * dk) // (Gk * dk)
    v_off = (2 * nk * dk) // (Gv * dv)
    z_off = conv_dim // (Gv * dv)
    in_specs = [pl.BlockSpec((1, TB, Gk * dk), lambda b, h, t: (b, t, h)),
                pl.BlockSpec((1, TB, Gk * dk), lambda b, h, t: (b, t, k_off + h)),
                pl.BlockSpec((1, TB, Gv * dv), lambda b, h, t: (b, t, v_off + h)),
                pl.BlockSpec((1, TB, Gv * dv), lambda b, h, t: (b, t, z_off + h)),
                pl.BlockSpec((1, 1, TB, 2 * Gv), lambda b, h, t: (b, h, t, 0)),
                pl.BlockSpec((1, 1, 2 * Gv, TB), lambda b, h, t: (b, h, 0, t))]
    args = [y, y, y, qkvz, gcol, grow]
    if s0 is not None:
        in_specs.append(pl.BlockSpec((1, Gv, dk, dv), lambda b, h, t: (b, h, 0, 0)))
        args.append(s0)
    in_specs.append(pl.BlockSpec((1, dv), lambda b, h, t: (0, 0)))
    args.append(g_onorm.reshape(1, dv).astype(F32))
    kern = functools.partial(_gdn_scan_kernel, C=C, cps=cps, Gv=Gv, dk=dk, dv=dv, rep=rep,
                             has_s0=s0 is not None)
    return pl.pallas_call(
        kern,
        grid=(B, HB, T // TB),
        in_specs=in_specs,
        out_specs=[pl.BlockSpec((1, TB, Gv * dv), lambda b, h, t: (b, t, h)),
                   pl.BlockSpec((1, Gv, dk, dv), lambda b, h, t: (b, h, 0, 0))],
        out_shape=[jax.ShapeDtypeStruct((B, T, nv * dv), BF16),
                   jax.ShapeDtypeStruct((B, nv, dk, dv), F32)],
        scratch_shapes=[pltpu.VMEM((Gv, dk, dv), F32)],
        compiler_params=_cparams("parallel", "parallel", "arbitrary"),
        name="gdn_scan",
    )(*args)


def _gdn_layer(h3, conv_buf, s0, gain, w_main, w_ba, conv_w, a_log, dt_bias, g_onorm, w_out, *, nk, nv, dk, dv):
    B, T, D = h3.shape
    M = B * T
    qk_dim = nk * dk
    conv_dim = 2 * qk_dim + nv * dv
    h2 = h3.reshape(M, D)
    qkvz = _mm(h2, [w_main], _ep_plain, [BF16], gain=gain, name="gdn_in_proj")[0].reshape(B, T, -1)
    pad = jnp.zeros((1, nv), F32)
    gates = _mm(h2, [w_ba], functools.partial(_gdn_gate_epilogue, nv=nv), [F32], gain=gain,
                extras=[jnp.concatenate([pad, a_log.reshape(1, nv)], axis=1),
                        jnp.concatenate([pad, dt_bias.reshape(1, nv)], axis=1)],
                name="gdn_gates")[0].reshape(B, T, 2 * nv)
    if conv_buf is None:
        buf8 = jnp.zeros((B, 8, conv_dim), F32)
    else:
        buf8 = jnp.pad(conv_buf, ((0, 0), (8 - conv_buf.shape[1], 0), (0, 0)))
    y, last8 = _gdn_conv(qkvz, buf8, conv_w, conv_dim=conv_dim, qk_dim=qk_dim, dk=dk)
    o, s_new = _gdn_scan(y, qkvz, gates, s0, g_onorm, nk=nk, nv=nv, dk=dk, dv=dv)
    h_new = _mm(o.reshape(M, nv * dv), [w_out], _ep_residual, [F32], extras=[h2], name="gdn_out_proj")[0]
    return h_new.reshape(B, T, D), last8[:, 5:, :], s_new


def _t5_bucket_np(rel):
    rel = np.asarray(rel, np.int32)
    half = NUM_BUCKETS // 2
    exact = half // 2
    base = np.where(rel > 0, half, 0)
    n = np.abs(rel)
    nf = np.maximum(n, 1).astype(np.float32)
    large = exact + (np.log(nf / np.float32(exact)) / np.float32(math.log(MAX_DISTANCE / exact))
                     * np.float32(half - exact)).astype(np.int32)
    large = np.minimum(large, half - 1)
    return base + np.where(n < exact, n, large)


def _lookup_kernel(oh_ref, tab_ref, o_ref):
    o_ref[...] = _dot_hi(oh_ref[...], tab_ref[...])


def _bias_lookup(buckets, rel_bias, far_bucket=None):
    flat = np.asarray(buckets).reshape(-1)
    nb, H = rel_bias.shape
    onehot = np.zeros((flat.size, nb), np.float32)
    onehot[np.arange(flat.size), flat] = 1.0
    if far_bucket is not None:
        onehot[:, far_bucket] -= 1.0
    R_ = flat.size
    tr = _pick_tile(R_, 2048)
    out = pl.pallas_call(
        _lookup_kernel,
        grid=(R_ // tr,),
        in_specs=[pl.BlockSpec((tr, nb), lambda i: (i, 0)), pl.BlockSpec((nb, H), lambda i: (0, 0))],
        out_specs=pl.BlockSpec((tr, H), lambda i: (i, 0)),
        out_shape=jax.ShapeDtypeStruct((R_, H), F32),
        compiler_params=_cparams("parallel"),
    )(jnp.asarray(onehot), rel_bias.astype(F32))
    return out.reshape(tuple(np.asarray(buckets).shape) + (H,))


def _sortable_key(x):
    bits = pltpu.bitcast(x, I32)
    return bits ^ ((bits >> 31) & 0x7FFFFFFF)


def _topk_threshold(keys_ref, n_tiles, tile, rows, topk, idx_bits):
    sub = tile // LANES

    def count(pred):
        def body(c, acc):
            cs = pl.multiple_of(c * tile, tile)
            for s in range(sub):
                kt = keys_ref[:, pl.ds(cs + s * LANES, LANES)]
                col = cs + s * LANES + lax.broadcasted_iota(I32, (rows, LANES), 1)
                acc = acc + pred(kt, col)
            return acc
        acc = lax.fori_loop(0, n_tiles, body, jnp.zeros((rows, LANES), F32))
        return jnp.sum(acc, axis=1, keepdims=True)

    def count_ge(cand):
        return count(lambda kt, col: jnp.where(kt >= cand, 1.0, 0.0))

    kf = float(topk)
    zero = jnp.zeros((rows, LANES), I32)
    lowest = jnp.full((rows, LANES), INT_MIN, I32)
    c_all = count_ge(lowest)
    c_pos = count_ge(zero)
    prefix = jnp.where(c_pos >= kf, zero, lowest)
    n_ge = jnp.where(c_pos >= kf, c_pos, c_all)

    def bit_cond(carry):
        it, _, n_ge = carry
        return jnp.logical_and(it < 31, jnp.max(n_ge) > kf)

    def bit_body(carry):
        it, prefix, n_ge = carry
        cand = prefix | jnp.left_shift(jnp.int32(1), 30 - it)
        cnt = count_ge(cand)
        ok = cnt >= kf
        return it + 1, jnp.where(ok, cand, prefix), jnp.where(ok, cnt, n_ge)

    _, thr, n_ge = lax.while_loop(bit_cond, bit_body, (jnp.int32(0), prefix, n_ge))

    @pl.when(jnp.max(n_ge) > kf)
    def _():
        n_gt = count(lambda kt, col: jnp.where(kt > thr, 1.0, 0.0))
        need = kf - n_gt

        def idx_body(it, p):
            cand = p | jnp.left_shift(jnp.int32(1), idx_bits - 1 - it)
            c = count(lambda kt, col: jnp.where(kt == thr, jnp.where(col < cand, 1.0, 0.0), 0.0))
            return jnp.where(c < need, cand, p)

        p = lax.fori_loop(0, idx_bits, idx_body, zero)

        def demote(c, carry):
            cs = pl.multiple_of(c * tile, tile)
            for s in range(sub):
                sl = pl.ds(cs + s * LANES, LANES)
                kt = keys_ref[:, sl]
                col = cs + s * LANES + lax.broadcasted_iota(I32, (rows, LANES), 1)
                keys_ref[:, sl] = jnp.where(kt == thr, jnp.where(col > p, kt - 1, kt), kt)
            return carry

        lax.fori_loop(0, n_tiles, demote, 0)

    return thr


def _dsa_prompt_kernel(q_ref, qi_ref, wi_ref, ki_ref, k_ref, v_ref, bias_ref, o_ref,
                       keys_ref, thr_ref, m_ref, acc_ref,
                       *, tq, ts, tf, topk, n_kv, rep, dh, idx_h, idx_d, chunk, idx_bits):
    i = pl.program_id(1)
    nvis = (i + 1) * tq
    n_st = (nvis + ts - 1) // ts
    row = lax.broadcasted_iota(I32, (tq, 1), 0)
    limit = i * tq + (row // chunk + 1) * chunk

    wi = wi_ref[0]
    qi = qi_ref[0]
    q_stack = jnp.concatenate([qi[:, h * idx_d:(h + 1) * idx_d] for h in range(idx_h)], axis=0)
    wis = [wi[:, h:h + 1] for h in range(idx_h)]

    def score_body(t, carry):
        ks = pl.multiple_of(t * ts, ts)
        kt = ki_ref[0, pl.ds(ks, ts), :]
        s_all = _dot_nt(q_stack, kt)
        sc = jnp.zeros((tq, ts), F32)
        for h in range(idx_h):
            sc = sc + wis[h] * jnp.maximum(s_all[h * tq:(h + 1) * tq], 0.0)
        sc = sc * (idx_d ** -0.5)
        col = ks + lax.broadcasted_iota(I32, (tq, ts), 1)
        sc = jnp.where(col < limit, sc, -jnp.inf)
        keys_ref[:, pl.ds(ks, ts)] = _sortable_key(sc)
        return carry

    lax.fori_loop(0, n_st, score_body, 0)

    thr_ref[...] = jnp.full((tq, LANES), INT_MIN, I32)

    @pl.when(nvis > topk)
    def _():
        thr_ref[...] = _topk_threshold(keys_ref, n_st, ts, tq, topk, idx_bits)

    thr = thr_ref[:, :1]
    q = q_ref[0]

    qg = [jnp.concatenate([q[:, (g * rep + r) * dh:(g * rep + r + 1) * dh] for r in range(rep)], axis=0)
          for g in range(n_kv)]
    m_ref[...] = jnp.full(m_ref.shape, NEG, F32)
    acc_ref[...] = jnp.zeros(acc_ref.shape, F32)

    def attend(ks, tk, bias_of, col_lo, col_hi):
        col = ks + lax.broadcasted_iota(I32, (tq, tk), 1)
        madd = jnp.where(keys_ref[:, pl.ds(ks, tk)] >= thr, 0.0, NEG)
        if col_lo is not None:
            madd = jnp.where(col >= col_lo, madd, NEG)
        if col_hi is not None:
            madd = jnp.where(col < col_hi, madd, NEG)
        madd = jnp.concatenate([madd] * rep, axis=0)
        ones = jnp.ones((tk, dh), BF16)
        s = []
        for g in range(n_kv):
            sg = _dot_nt(qg[g], k_ref[0, pl.ds(ks, tk), g * dh:(g + 1) * dh]) + madd
            s.append(sg if bias_of is None else sg + bias_of(g))
        for g in range(n_kv):
            m_prev = m_ref[g]
            m_next = jnp.maximum(m_prev, jnp.max(s[g], axis=1, keepdims=True))
            alpha = jnp.exp2(m_prev - m_next)
            p = jnp.exp2(s[g] - jnp.concatenate([m_next] * (tk // LANES), axis=1)).astype(BF16)
            vt = jnp.concatenate([v_ref[0, pl.ds(ks, tk), g * dh:(g + 1) * dh], ones], axis=1)
            acc_ref[g] = jnp.concatenate([alpha] * (2 * dh // LANES), axis=1) * acc_ref[g] + _dot(p, vt)
            m_ref[g] = m_next

    far_end = jnp.maximum(i - 1, 0) * tq

    def far_body(t, carry):
        attend(pl.multiple_of(t * tf, tf), tf, None, None, far_end)
        return carry

    lax.fori_loop(0, (far_end + tf - 1) // tf, far_body, 0)

    @pl.when(i == 0)
    def _():
        attend(0, tq, lambda g: bias_ref[g, :, tq:], None, limit)

    @pl.when(i > 0)
    def _():
        attend(pl.multiple_of((i - 1) * tq, tq), 2 * tq, lambda g: bias_ref[g], None, limit)

    for g in range(n_kv):
        acc = acc_ref[g]
        out = acc[:, :dh] / acc[:, dh:]
        for r in range(rep):
            hcol = (g * rep + r) * dh
            o_ref[0, :, hcol:hcol + dh] = out[r * tq:(r + 1) * tq].astype(o_ref.dtype)


def _dsa_prompt(qa, wi, ki, kb, vb, rel_bias, *, n_heads, n_kv, dh, idx_h, idx_d, tq=128):
    B, T, _ = qa.shape
    assert T % tq == 0 and tq % CHUNK == 0
    topk = min(TOPK_MAX, T // 4)
    rep = n_heads // n_kv
    qd = n_heads * dh
    idd = idx_h * idx_d
    assert qd % idd == 0
    ts = _pick_tile(T, 512)
    far_bucket = int(_t5_bucket_np(np.array([-(tq + 1)]))[0])
    assert np.all(_t5_bucket_np(-np.arange(tq + 1, 4 * T)) == far_bucket)
    dq = np.arange(tq)[:, None]
    buckets = _t5_bucket_np(np.arange(2 * tq)[None, :] - tq - dq)
    bias = _bias_lookup(buckets, rel_bias, far_bucket) * LOG2E
    bias = jnp.transpose(bias.reshape(tq, 2 * tq, n_kv, rep), (2, 3, 0, 1)).reshape(n_kv, rep * tq, 2 * tq)
    kern = functools.partial(_dsa_prompt_kernel, tq=tq, ts=ts, tf=ts, topk=topk, n_kv=n_kv, rep=rep, dh=dh,
                             idx_h=idx_h, idx_d=idx_d, chunk=CHUNK, idx_bits=max(1, (T - 1).bit_length()))
    return pl.pallas_call(
        kern,
        grid=(B, T // tq),
        in_specs=[pl.BlockSpec((1, tq, qd), lambda b, i: (b, i, 0)),
                  pl.BlockSpec((1, tq, idd), lambda b, i: (b, i, qd // idd)),
                  pl.BlockSpec((1, tq, idx_h), lambda b, i: (b, i, 0)),
                  pl.BlockSpec((1, T, idx_d), lambda b, i: (b, 0, 0)),
                  pl.BlockSpec((1, T, n_kv * dh), lambda b, i: (b, 0, 0)),
                  pl.BlockSpec((1, T, n_kv * dh), lambda b, i: (b, 0, 0)),
                  pl.BlockSpec((n_kv, rep * tq, 2 * tq), lambda b, i: (0, 0, 0))],
        out_specs=pl.BlockSpec((1, tq, qd), lambda b, i: (b, i, 0)),
        out_shape=jax.ShapeDtypeStruct((B, T, qd), BF16),
        scratch_shapes=[pltpu.VMEM((tq, T), I32), pltpu.VMEM((tq, LANES), I32),
                        pltpu.VMEM((n_kv, rep * tq, LANES), F32), pltpu.VMEM((n_kv, rep * tq, 2 * dh), F32)],
        compiler_params=_cparams("parallel", "arbitrary"),
        name="dsa_prompt",
    )(qa, qa, wi, ki, kb, vb, bias)


def _dsa_sample_kernel(q_ref, qi_ref, wi_ref, ki_ref, k_ref, v_ref, bias_ref, o_ref, keys_ref,
                       *, tq, S, n_valid, topk, n_kv, rep, dh, idx_h, idx_d, idx_bits):
    wi = wi_ref[0]
    qi = qi_ref[0]
    kt = ki_ref[0]
    sc = jnp.zeros((tq, S), F32)
    for h in range(idx_h):
        sc = sc + wi[:, h:h + 1] * jnp.maximum(_dot_nt(qi[:, h * idx_d:(h + 1) * idx_d], kt), 0.0)
    sc = sc * (idx_d ** -0.5)
    col = lax.broadcasted_iota(I32, (tq, S), 1)
    valid = col < n_valid
    keys_ref[...] = _sortable_key(jnp.where(valid, sc, -jnp.inf))
    thr = _topk_threshold(keys_ref, S // LANES, LANES, tq, topk, idx_bits)[:, :1]
    madd = jnp.where(keys_ref[...] >= thr, jnp.where(valid, 0.0, NEG), NEG)
    madd = jnp.concatenate([madd] * rep, axis=0)
    q = q_ref[0]
    for g in range(n_kv):
        qg = jnp.concatenate([q[:, (g * rep + r) * dh:(g * rep + r + 1) * dh] for r in range(rep)], axis=0)
        s = _dot_nt(qg, k_ref[0, :, g * dh:(g + 1) * dh]) + bias_ref[g] + madd
        p = jnp.exp2(s - jnp.max(s, axis=1, keepdims=True))
        out = _dot(p.astype(BF16), v_ref[0, :, g * dh:(g + 1) * dh]) / jnp.sum(p, axis=1, keepdims=True)
        for r in range(rep):
            hcol = (g * rep + r) * dh
            o_ref[0, :, hcol:hcol + dh] = out[r * tq:(r + 1) * tq].astype(o_ref.dtype)


def _dsa_sample(qa, wi, ki_all, k_all, v_all, rel_bias, *, past, n_heads, n_kv, dh, idx_h, idx_d):
    B, tq, _ = qa.shape
    S = k_all.shape[1]
    n_valid = past + tq
    topk = min(TOPK_MAX, n_valid // 4)
    rep = n_heads // n_kv
    qd = n_heads * dh
    idd = idx_h * idx_d
    rel = np.minimum(np.arange(S), n_valid - 1)[None, :] - (past + np.arange(tq))[:, None]
    bias = _bias_lookup(_t5_bucket_np(rel), rel_bias) * LOG2E
    bias = jnp.transpose(bias.reshape(tq, S, n_kv, rep), (2, 3, 0, 1)).reshape(n_kv, rep * tq, S)
    kern = functools.partial(_dsa_sample_kernel, tq=tq, S=S, n_valid=n_valid, topk=topk, n_kv=n_kv, rep=rep,
                             dh=dh, idx_h=idx_h, idx_d=idx_d, idx_bits=max(1, (S - 1).bit_length()))
    return pl.pallas_call(
        kern,
        grid=(B,),
        in_specs=[pl.BlockSpec((1, tq, qd), lambda b: (b, 0, 0)),
                  pl.BlockSpec((1, tq, idd), lambda b: (b, 0, qd // idd)),
                  pl.BlockSpec((1, tq, idx_h), lambda b: (b, 0, 0)),
                  pl.BlockSpec((1, S, idx_d), lambda b: (b, 0, 0)),
                  pl.BlockSpec((1, S, n_kv * dh), lambda b: (b, 0, 0)),
                  pl.BlockSpec((1, S, n_kv * dh), lambda b: (b, 0, 0)),
                  pl.BlockSpec((n_kv, rep * tq, S), lambda b: (0, 0, 0))],
        out_specs=pl.BlockSpec((1, tq, qd), lambda b: (b, 0, 0)),
        out_shape=jax.ShapeDtypeStruct((B, tq, qd), BF16),
        scratch_shapes=[pltpu.VMEM((tq, S), I32)],
        compiler_params=_cparams("parallel"),
        name="dsa_sample",
    )(qa, qa, wi, ki_all, k_all, v_all, bias)


def _dsa_tail_epilogue(accs, extras, j, *, idx_d, idx_h):
    a = accs[0]
    ki = _rms(a[:, :idx_d], extras[0][:, :idx_d])
    return [jnp.concatenate([ki, a[:, idx_d:] * idx_h ** -0.5], axis=1)]


def _dsa_layer(h3, cache, gain, w_qa, w_kv, w_tail, g_q, g_k, g_ki, w_out, rel_bias,
               *, n_heads, n_kv, dh, idx_h, idx_d):
    B, T, D = h3.shape
    M = B * T
    qd = n_heads * dh
    kvd = n_kv * dh
    h2 = h3.reshape(M, D)
    tn = 512
    g_qa = jnp.concatenate([jnp.tile(g_q, n_heads), jnp.ones((idx_h * idx_d,), F32)]).reshape(1, -1)
    qa = _mm(h2, [w_qa], functools.partial(_ep_headnorm, dh=dh, n_norm_tiles=qd // tn,
                                           post_scale=dh ** -0.5 * LOG2E), [BF16],
             gain=gain, extras=[g_qa], tn=tn, name="dsa_q_proj")[0].reshape(B, T, -1)
    g_kv = jnp.concatenate([jnp.tile(g_k, n_kv), jnp.ones((kvd,), F32)]).reshape(1, -1)
    kv = _mm(h2, [w_kv], functools.partial(_ep_headnorm, dh=dh, n_norm_tiles=kvd // tn), [F32],
             gain=gain, extras=[g_kv], tn=tn, name="dsa_kv_proj")[0]
    g_tail = jnp.concatenate([g_ki, jnp.ones((idx_h,), F32)]).reshape(1, -1)
    tail = _mm(h2, [w_tail], functools.partial(_dsa_tail_epilogue, idx_d=idx_d, idx_h=idx_h), [F32],
               gain=gain, extras=[g_tail], name="dsa_idx_proj")[0]
    k_new = kv[:, :kvd].reshape(B, T, kvd)
    v_new = kv[:, kvd:].reshape(B, T, kvd)
    ki_new = tail[:, :idx_d].reshape(B, T, idx_d)
    wi = tail[:, idx_d:].reshape(B, T, idx_h)
    dims = dict(n_heads=n_heads, n_kv=n_kv, dh=dh, idx_h=idx_h, idx_d=idx_d)
    if cache is None:
        att = _dsa_prompt(qa, wi, ki_new.astype(BF16), k_new.astype(BF16), v_new.astype(BF16), rel_bias, **dims)
    else:
        ck, cv, cki = cache
        past = ck.shape[1]
        S = -(-(past + T) // LANES) * LANES
        padr = ((0, 0), (0, S - past - T), (0, 0))
        k_all = jnp.pad(jnp.concatenate([ck.reshape(B, past, kvd), k_new], axis=1).astype(BF16), padr)
        v_all = jnp.pad(jnp.concatenate([cv.reshape(B, past, kvd), v_new], axis=1).astype(BF16), padr)
        ki_all = jnp.pad(jnp.concatenate([cki, ki_new], axis=1).astype(BF16), padr)
        att = _dsa_sample(qa, wi, ki_all, k_all, v_all, rel_bias, past=past, **dims)
    h_new = _mm(att.reshape(M, qd), [w_out], _ep_residual, [F32], extras=[h2], name="dsa_out_proj")[0]
    return (h_new.reshape(B, T, D), k_new.reshape(B, T, n_kv, dh), v_new.reshape(B, T, n_kv, dh), ki_new)


def _router_kernel(h_ref, g_ref, wr_ref, xn_ref, ro_ref, *, n_exp):
    x = _rms(h_ref[...], g_ref[...])
    xn_ref[...] = x.astype(BF16)
    logits = _dot_hi(x, wr_ref[...])
    lane = lax.broadcasted_iota(I32, logits.shape, 1).astype(F32)
    lg = jnp.where(lane < n_exp, logits, -jnp.inf)
    m1 = jnp.max(lg, axis=1, keepdims=True)
    i1 = jnp.min(jnp.where(lg == m1, lane, float(LANES)), axis=1, keepdims=True)
    lg2 = jnp.where(lane == i1, -jnp.inf, lg)
    m2 = jnp.max(lg2, axis=1, keepdims=True)
    i2 = jnp.min(jnp.where(lg2 == m2, lane, float(LANES)), axis=1, keepdims=True)
    e = jnp.exp(m2 - m1)
    g1 = 1.0 / (1.0 + e)
    g2 = e / (1.0 + e)
    ro_ref[...] = jnp.where(lane == 0, i1, jnp.where(lane == 1, i2, jnp.where(lane == 2, g1,
                            jnp.where(lane == 3, g2, 0.0))))


def _router(h2, gain, w_router, *, tm=512):
    M, D = h2.shape
    n_exp = w_router.shape[1]
    tm = _pick_tile(M, tm)
    wr = jnp.pad(w_router.astype(F32), ((0, 0), (0, LANES - n_exp)))
    return pl.pallas_call(
        functools.partial(_router_kernel, n_exp=n_exp),
        grid=(M // tm,),
        in_specs=[pl.BlockSpec((tm, D), lambda i: (i, 0)), pl.BlockSpec((1, D), lambda i: (0, 0)),
                  pl.BlockSpec((D, LANES), lambda i: (0, 0))],
        out_specs=[pl.BlockSpec((tm, D), lambda i: (i, 0)), pl.BlockSpec((tm, LANES), lambda i: (i, 0))],
        out_shape=[jax.ShapeDtypeStruct((M, D), BF16), jax.ShapeDtypeStruct((M, LANES), F32)],
        compiler_params=_cparams("parallel"),
        name="moe_router",
    )(h2, gain.reshape(1, D).astype(F32), wr)


def _moe(h_list, gain, w_router, wg, wu, wd, *, bm=512):
    n_exp = w_router.shape[1]
    routed = [_router(h, gain, w_router) for h in h_list]
    xn = jnp.concatenate([r[0] for r in routed], axis=0)
    ro = jnp.concatenate([r[1] for r in routed], axis=0)
    N, D = xn.shape
    A = N * MOE_TOP_K
    flat_e = ro[:, :MOE_TOP_K].astype(I32).reshape(-1)
    onehot = (flat_e[:, None] == jnp.arange(n_exp, dtype=I32)[None, :]).astype(I32)
    csum = jnp.cumsum(onehot, axis=0)
    rank = jnp.take_along_axis(csum, flat_e[:, None], axis=1)[:, 0] - 1
    counts = csum[-1]
    padded = (counts + bm - 1) // bm * bm
    pad_end = jnp.cumsum(padded)
    dest = (pad_end - padded)[flat_e] + rank
    n_blocks = (A + n_exp * (bm - 1)) // bm
    row_src = jnp.zeros((n_blocks * bm,), I32).at[dest].set(jnp.arange(A, dtype=I32) // MOE_TOP_K)
    block_e = jnp.minimum(jnp.searchsorted(pad_end, jnp.arange(n_blocks, dtype=I32) * bm, side='right'),
                          n_exp - 1).astype(I32)
    n_used = (pad_end[-1:] // bm).astype(I32)
    xs = xn.at[row_src].get(mode="promise_in_bounds")
    yp = _ffn_experts(xs, block_e, n_used, wg, wu, wd, bm=bm)
    dest2 = dest.reshape(N, MOE_TOP_K)
    y0 = yp.at[dest2[:, 0]].get(mode="promise_in_bounds")
    y1 = yp.at[dest2[:, 1]].get(mode="promise_in_bounds")
    return y0, y1, ro


def kernel(x_prompt, x_sample, state_gdn, state_gdn_conv, cache_dsa_k, cache_dsa_v, cache_dsa_kidx, p_prompt, p_sample, g_mix, g_ffn, g_ple, w_ple_in, w_ple_gate, w_in_gdn, conv_w_gdn, a_log_gdn, dt_bias_gdn, g_onorm_gdn, w_out_gdn, w_in_dsa, g_qnorm_dsa, g_knorm_dsa, g_kidx_norm_dsa, w_out_dsa, rel_bias, w_gate_ffn, w_up_ffn, w_down_ffn, w_router, w_gate_moe, w_up_moe, w_down_moe):
    depth = g_mix.shape[0]
    Bp, Tp, D = x_prompt.shape
    Bs, Ts, _ = x_sample.shape
    nv = a_log_gdn.shape[1]
    dk, dv = state_gdn.shape[3], state_gdn.shape[4]
    conv_dim = conv_w_gdn.shape[2]
    nk = (conv_dim - nv * dv) // (2 * dk)
    gdn_dims = dict(nk=nk, nv=nv, dk=dk, dv=dv)
    n_heads = rel_bias.shape[1]
    n_kv, dh = cache_dsa_k.shape[3], cache_dsa_k.shape[4]
    idx_d = cache_dsa_kidx.shape[3]
    qd, kvd = n_heads * dh, n_kv * dh
    idx_h = (w_in_dsa.shape[2] - qd - 2 * kvd - idx_d) // (idx_d + 1)
    dsa_dims = dict(n_heads=n_heads, n_kv=n_kv, dh=dh, idx_h=idx_h, idx_d=idx_d)

    hp, hs = x_prompt, x_sample
    outs = {k: [] for k in ("sp", "cp", "ss", "cs", "kp", "vp", "kip", "ks", "vs", "kis")}
    for i in range(depth):
        j = i // 2
        if i % 2 == 0:
            w_in = w_in_gdn[j]
            n_main = conv_dim + nv * dv
            wa = (g_mix[i], w_in[:, :n_main].astype(BF16), w_in[:, n_main:].astype(BF16), conv_w_gdn[j],
                  a_log_gdn[j], dt_bias_gdn[j], g_onorm_gdn[j], w_out_gdn[j].astype(BF16))
            hp, cbp, sp = _gdn_layer(hp, None, None, *wa, **gdn_dims)
            hs, cbs, ss = _gdn_layer(hs, state_gdn_conv[j], state_gdn[j], *wa, **gdn_dims)
            outs["sp"].append(sp); outs["cp"].append(cbp); outs["ss"].append(ss); outs["cs"].append(cbs)
            wf = (g_ffn[i], w_gate_ffn[j].astype(BF16), w_up_ffn[j].astype(BF16), w_down_ffn[j].astype(BF16))
            hp = _ffn_dense(hp.reshape(Bp * Tp, D), *wf).reshape(Bp, Tp, D)
            hs = _ffn_dense(hs.reshape(Bs * Ts, D), *wf).reshape(Bs, Ts, D)
            moe_p = moe_s = None
        else:
            w_in = w_in_dsa[j]
            o1 = qd + 2 * kvd
            o2 = o1 + idx_h * idx_d
            w_qa = jnp.concatenate([w_in[:, :qd], w_in[:, o1:o2]], axis=1).astype(BF16)
            wb = (g_mix[i], w_qa, w_in[:, qd:o1].astype(BF16), w_in[:, o2:].astype(BF16), g_qnorm_dsa[j],
                  g_knorm_dsa[j], g_kidx_norm_dsa[j], w_out_dsa[j].astype(BF16), rel_bias)
            hp, k, v, ki = _dsa_layer(hp, None, *wb, **dsa_dims)
            outs["kp"].append(k); outs["vp"].append(v); outs["kip"].append(ki)
            hs, k, v, ki = _dsa_layer(hs, (cache_dsa_k[j], cache_dsa_v[j], cache_dsa_kidx[j]), *wb, **dsa_dims)
            outs["ks"].append(k); outs["vs"].append(v); outs["kis"].append(ki)
            y0, y1, ro = _moe([hp.reshape(Bp * Tp, D), hs.reshape(Bs * Ts, D)], g_ffn[i], w_router[j],
                              w_gate_moe[j].astype(BF16), w_up_moe[j].astype(BF16), w_down_moe[j].astype(BF16))
            moe_p, moe_s = (y0, y1, ro, 0), (y0, y1, ro, Bp * Tp)
        wpl = (g_ple[i], w_ple_in[i].astype(BF16), w_ple_gate[i].astype(BF16))
        hp = _ple_add(hp.reshape(Bp * Tp, D), p_prompt[i].reshape(Bp * Tp, -1), *wpl, moe=moe_p).reshape(Bp, Tp, D)
        hs = _ple_add(hs.reshape(Bs * Ts, D), p_sample[i].reshape(Bs * Ts, -1), *wpl, moe=moe_s).reshape(Bs, Ts, D)
    st = lambda name: jnp.stack(outs[name])
    return (hp, hs, st("sp"), st("cp"), st("kp"), st("vp"), st("kip"),
            st("ss"), st("cs"), st("ks"), st("vs"), st("kis"))
```

```python
import functools
import math

import numpy as np
import jax
import jax.numpy as jnp
from jax import lax
from jax.experimental import pallas as pl
from jax.experimental.pallas import tpu as pltpu

F32 = jnp.float32
BF16 = jnp.bfloat16
I32 = jnp.int32

EPS = 1e-6
CHUNK = 64
TOPK_MAX = 256
MOE_TOP_K = 2
NUM_BUCKETS = 32
MAX_DISTANCE = 128
NEG = -1e30
INT_MIN = -(2 ** 31)
LOG2E = math.log2(math.e)
HI = lax.Precision.HIGHEST

V7X_VMEM_LIMIT = 56 * 1024 * 1024
LANES = 128


def _cparams(*sem):
    return pltpu.CompilerParams(dimension_semantics=sem, vmem_limit_bytes=V7X_VMEM_LIMIT)


def _sigmoid(x):
    return 1.0 / (1.0 + jnp.exp(-x))


def _silu(x):
    return x * _sigmoid(x)


def _rms(x, g):
    return x * lax.rsqrt(jnp.mean(x * x, axis=-1, keepdims=True) + EPS) * g


def _dot(a, b):
    return jnp.dot(a, b, preferred_element_type=F32)


def _dot_nt(a, b):
    return lax.dot_general(a, b, (((1,), (1,)), ((), ())), preferred_element_type=F32)


def _dot_tn(a, b):
    return lax.dot_general(a, b, (((0,), (0,)), ((), ())), preferred_element_type=F32)


def _dot_hi(a, b):
    return jnp.dot(a, b, preferred_element_type=F32, precision=HI)


def _pick_tile(n, pref):
    t = min(n, pref)
    while n % t:
        t //= 2
    return t


def _mm_kernel(*refs, has_gain, n_w, n_extra, n_out, epilogue):
    x_ref = refs[0]
    pos = 1
    g_ref = None
    if has_gain:
        g_ref = refs[pos]
        pos += 1
    w_refs = refs[pos:pos + n_w]
    pos += n_w
    e_refs = refs[pos:pos + n_extra]
    pos += n_extra
    o_refs = refs[pos:pos + n_out]
    pos += n_out
    xs_ref = refs[pos]

    @pl.when(pl.program_id(1) == 0)
    def _():
        x = x_ref[...].astype(F32)
        if has_gain:
            x = _rms(x, g_ref[...])
        xs_ref[...] = x.astype(BF16)

    xs = xs_ref[...]
    accs = [_dot(xs, w[...]) for w in w_refs]
    outs = epilogue(accs, [e[...] for e in e_refs], pl.program_id(1))
    for o_ref, o in zip(o_refs, outs):
        o_ref[...] = o.astype(o_ref.dtype)


def _mm(x, ws, epilogue, out_dtypes, *, gain=None, extras=(), tm=1024, tn=512, name=None):
    M, K = x.shape
    N = ws[0].shape[1]
    tm = _pick_tile(M, tm)
    tn = _pick_tile(N, tn)
    in_specs = [pl.BlockSpec((tm, K), lambda i, j: (i, 0))]
    args = [x]
    if gain is not None:
        in_specs.append(pl.BlockSpec((1, K), lambda i, j: (0, 0)))
        args.append(gain.reshape(1, K).astype(F32))
    for w in ws:
        in_specs.append(pl.BlockSpec((K, tn), lambda i, j: (0, j)))
        args.append(w)
    for e in extras:
        if e.shape[0] == 1:
            in_specs.append(pl.BlockSpec((1, tn), lambda i, j: (0, j)))
        else:
            in_specs.append(pl.BlockSpec((tm, tn), lambda i, j: (i, j)))
        args.append(e)
    out_shape = [jax.ShapeDtypeStruct((M, N), d) for d in out_dtypes]
    out_specs = [pl.BlockSpec((tm, tn), lambda i, j: (i, j)) for _ in out_dtypes]
    kern = functools.partial(_mm_kernel, has_gain=gain is not None, n_w=len(ws), n_extra=len(extras),
                             n_out=len(out_dtypes), epilogue=epilogue)
    return pl.pallas_call(
        kern,
        grid=(M // tm, N // tn),
        in_specs=in_specs,
        out_specs=out_specs,
        out_shape=out_shape,
        scratch_shapes=[pltpu.VMEM((tm, K), BF16)],
        compiler_params=_cparams("parallel", "arbitrary"),
        name=name,
    )(*args)


def _ep_plain(accs, extras, j):
    return [accs[0]]


def _ep_residual(accs, extras, j):
    return [extras[0] + accs[0]]


def _ep_headnorm(accs, extras, j, *, dh, n_norm_tiles, post_scale=1.0):
    a = accs[0]
    g = extras[0]
    cols = []
    for h in range(a.shape[1] // dh):
        sl = slice(h * dh, (h + 1) * dh)
        cols.append(_rms(a[:, sl], g[:, sl]) * post_scale)
    normed = jnp.concatenate(cols, axis=1) if len(cols) > 1 else cols[0]
    return [jnp.where(j < n_norm_tiles, normed, a)]


def _ffn_kernel(*refs, has_gain, has_be):
    if has_be:
        used = pl.program_id(0) < refs[1][0]
        pl.when(used)(lambda: _ffn_body(refs[2:], has_gain))
        o_ref = refs[6]

        @pl.when(jnp.logical_not(used))
        def _():
            o_ref[...] = jnp.zeros_like(o_ref)
    else:
        _ffn_body(refs, has_gain)


def _ffn_body(refs, has_gain):
    pos = 0
    x_ref = refs[pos]
    pos += 1
    g_ref = None
    if has_gain:
        g_ref = refs[pos]
        pos += 1
    wg_ref, wu_ref, wd_ref, o_ref, scr_ref = refs[pos:pos + 5]
    j = pl.program_id(1)
    if has_gain:
        xs_ref, acc_ref = scr_ref, o_ref
    else:
        xs_ref, acc_ref = x_ref, scr_ref

    @pl.when(j == 0)
    def _():
        if has_gain:
            x = x_ref[...].astype(F32)
            xs_ref[...] = _rms(x, g_ref[...]).astype(BF16)
            acc_ref[...] = x
        else:
            acc_ref[...] = jnp.zeros_like(acc_ref)

    xs = xs_ref[...]
    wg = wg_ref[...].reshape(wg_ref.shape[-2:])
    wu = wu_ref[...].reshape(wu_ref.shape[-2:])
    wd = wd_ref[...].reshape(wd_ref.shape[-2:])
    a = (_silu(_dot(xs, wg)) * _dot(xs, wu)).astype(BF16)
    acc_ref[...] += _dot(a, wd)

    if not has_gain:
        @pl.when(j == pl.num_programs(1) - 1)
        def _():
            o_ref[...] = acc_ref[...].astype(o_ref.dtype)


def _ffn_dense(h, gain, wg, wu, wd, *, tm=1024, tf=512):
    M, D = h.shape
    Fdim = wg.shape[1]
    tm = _pick_tile(M, tm)
    tf = _pick_tile(Fdim, tf)
    return pl.pallas_call(
        functools.partial(_ffn_kernel, has_gain=True, has_be=False),
        grid=(M // tm, Fdim // tf),
        in_specs=[pl.BlockSpec((tm, D), lambda i, j: (i, 0)),
                  pl.BlockSpec((1, D), lambda i, j: (0, 0)),
                  pl.BlockSpec((D, tf), lambda i, j: (0, j)),
                  pl.BlockSpec((D, tf), lambda i, j: (0, j)),
                  pl.BlockSpec((tf, D), lambda i, j: (j, 0))],
        out_specs=pl.BlockSpec((tm, D), lambda i, j: (i, 0)),
        out_shape=jax.ShapeDtypeStruct((M, D), F32),
        scratch_shapes=[pltpu.VMEM((tm, D), BF16)],
        compiler_params=_cparams("parallel", "arbitrary"),
        name="ffn_dense",
    )(h, gain.reshape(1, D).astype(F32), wg, wu, wd)


def _ffn_experts(xs, block_e, n_used, wg, wu, wd, *, bm, tf=512):
    P, D = xs.shape
    Fdim = wg.shape[2]
    tf = _pick_tile(Fdim, tf)
    nf = Fdim // tf

    def row(i, nu):
        return jnp.minimum(i, nu[0] - 1)

    def col(i, j, nu):
        return jnp.where(i < nu[0], j, nf - 1)

    grid_spec = pltpu.PrefetchScalarGridSpec(
        num_scalar_prefetch=2,
        grid=(P // bm, nf),
        in_specs=[pl.BlockSpec((bm, D), lambda i, j, be, nu: (row(i, nu), 0)),
                  pl.BlockSpec((1, D, tf), lambda i, j, be, nu: (be[row(i, nu)], 0, col(i, j, nu))),
                  pl.BlockSpec((1, D, tf), lambda i, j, be, nu: (be[row(i, nu)], 0, col(i, j, nu))),
                  pl.BlockSpec((1, tf, D), lambda i, j, be, nu: (be[row(i, nu)], col(i, j, nu), 0))],
        out_specs=pl.BlockSpec((bm, D), lambda i, j, be, nu: (i, 0)),
        scratch_shapes=[pltpu.VMEM((bm, D), F32)],
    )
    return pl.pallas_call(
        functools.partial(_ffn_kernel, has_gain=False, has_be=True),
        grid_spec=grid_spec,
        out_shape=jax.ShapeDtypeStruct((P, D), BF16),
        compiler_params=_cparams("parallel", "arbitrary"),
        name="moe_experts",
    )(block_e, n_used, xs, wg, wu, wd)


def _ple_kernel(*refs, has_moe, tn):
    if has_moe:
        h_ref, y0_ref, y1_ref, ro_ref, g_ref, wgate_ref, p_ref, wp_ref, o_ref, xs_ref, hs_ref = refs
    else:
        h_ref, g_ref, wgate_ref, p_ref, wp_ref, o_ref, xs_ref, hs_ref = refs
    j = pl.program_id(1)

    @pl.when(j == 0)
    def _():
        h = h_ref[...]
        if has_moe:
            ro = ro_ref[...]
            h = h + (ro[:, MOE_TOP_K:MOE_TOP_K + 1] * y0_ref[...].astype(F32)
                     + ro[:, MOE_TOP_K + 1:MOE_TOP_K + 2] * y1_ref[...].astype(F32))
        hs_ref[...] = h
        xs_ref[...] = _rms(h, g_ref[...]).astype(BF16)

    gate = _sigmoid(_dot(xs_ref[...], wgate_ref[...]))
    ht = hs_ref[:, pl.ds(pl.multiple_of(j * tn, tn), tn)]
    o_ref[...] = ht + gate * _dot(p_ref[...].astype(BF16), wp_ref[...])


def _ple_add(h, p, gain, wp, wgate, *, moe=None, tm=1024, tn=1024):
    M, D = h.shape
    Pd = p.shape[1]
    tm = _pick_tile(M, tm if moe is None else 512)
    tn = _pick_tile(D, tn)
    in_specs = [pl.BlockSpec((tm, D), lambda i, j: (i, 0))]
    args = [h]
    if moe is not None:
        y0, y1, ro, first_row = moe
        assert first_row % tm == 0
        off = first_row // tm
        in_specs += [pl.BlockSpec((tm, D), lambda i, j: (i + off, 0)),
                     pl.BlockSpec((tm, D), lambda i, j: (i + off, 0)),
                     pl.BlockSpec((tm, LANES), lambda i, j: (i + off, 0))]
        args += [y0, y1, ro]
    in_specs += [pl.BlockSpec((1, D), lambda i, j: (0, 0)),
                 pl.BlockSpec((D, tn), lambda i, j: (0, j)),
                 pl.BlockSpec((tm, Pd), lambda i, j: (i, 0)),
                 pl.BlockSpec((Pd, tn), lambda i, j: (0, j))]
    args += [gain.reshape(1, D).astype(F32), wgate, p, wp]
    return pl.pallas_call(
        functools.partial(_ple_kernel, has_moe=moe is not None, tn=tn),
        grid=(M // tm, D // tn),
        in_specs=in_specs,
        out_specs=pl.BlockSpec((tm, tn), lambda i, j: (i, j)),
        out_shape=jax.ShapeDtypeStruct((M, D), F32),
        scratch_shapes=[pltpu.VMEM((tm, D), BF16), pltpu.VMEM((tm, D), F32)],
        compiler_params=_cparams("parallel", "arbitrary"),
        name="ple_add",
    )(*args)


def _gdn_gate_epilogue(accs, extras, j, *, nv):
    ba = accs[0]
    a_log, dt = extras
    beta = _sigmoid(ba[:, :nv])
    x = ba[:, nv:] + dt[:, nv:]
    softplus = jnp.maximum(x, 0.0) + jnp.log(1.0 + jnp.exp(-jnp.abs(x)))
    g = -jnp.exp(a_log[:, nv:]) * softplus
    return [jnp.concatenate([beta, g], axis=1)]


def _gdn_conv_kernel(x_ref, buf_ref, w_ref, y_ref, cs_ref, xe_ref, *, tt, tc, dk, n_q_blocks, n_qk_blocks):
    cb = pl.program_id(1)
    t = pl.program_id(2)

    @pl.when(t == 0)
    def _():
        xe_ref[0:8, :] = buf_ref[0]

    @pl.when(t > 0)
    def _():
        xe_ref[0:8, :] = xe_ref[tt:tt + 8, :]

    x = x_ref[0]
    xf = x.astype(F32)
    xe_ref[8:8 + tt, :] = xf
    w = w_ref[...]
    rows = lax.broadcasted_iota(I32, (tt, tt), 0)
    cols = lax.broadcasted_iota(I32, (tt, tt), 1)
    acc = w[3:4] * xf
    for j in range(1, 4):
        shift = jnp.where(rows - cols == j, 1.0, 0.0).astype(BF16)
        acc = acc + w[3 - j:4 - j] * _dot(shift, x)
    head = (w[3:4] * xe_ref[8:16, :] + w[2:3] * xe_ref[7:15, :]
            + w[1:2] * xe_ref[6:14, :] + w[0:1] * xe_ref[5:13, :])
    y = _silu(jnp.concatenate([head, acc[8:]], axis=0))
    cs_ref[0] = xe_ref[tt:tt + 8, :]

    @pl.when(cb < n_qk_blocks)
    def _():
        scale = jnp.where(cb < n_q_blocks, dk ** -0.5, 1.0).astype(F32)
        for h in range(tc // dk):
            yh = y[:, h * dk:(h + 1) * dk]
            y_ref[0, :, h * dk:(h + 1) * dk] = (
                yh * lax.rsqrt(jnp.sum(yh * yh, axis=-1, keepdims=True) + EPS) * scale).astype(y_ref.dtype)

    @pl.when(cb >= n_qk_blocks)
    def _():
        y_ref[0] = y.astype(y_ref.dtype)


def _gdn_conv(qkvz, buf8, conv_w, *, conv_dim, qk_dim, dk, tt=256, tc=1024):
    B, T, _ = qkvz.shape
    tt = _pick_tile(T, tt)
    assert tt >= 8 and conv_dim % tc == 0 and qk_dim % tc == 0 and qkvz.dtype == BF16
    kern = functools.partial(_gdn_conv_kernel, tt=tt, tc=tc, dk=dk, n_q_blocks=qk_dim // tc,
                             n_qk_blocks=2 * qk_dim // tc)
    return pl.pallas_call(
        kern,
        grid=(B, conv_dim // tc, T // tt),
        in_specs=[pl.BlockSpec((1, tt, tc), lambda b, c, t: (b, t, c)),
                  pl.BlockSpec((1, 8, tc), lambda b, c, t: (b, 0, c)),
                  pl.BlockSpec((4, tc), lambda b, c, t: (0, c))],
        out_specs=[pl.BlockSpec((1, tt, tc), lambda b, c, t: (b, t, c)),
                   pl.BlockSpec((1, 8, tc), lambda b, c, t: (b, 0, c))],
        out_shape=[jax.ShapeDtypeStruct((B, T, conv_dim), BF16),
                   jax.ShapeDtypeStruct((B, 8, conv_dim), F32)],
        scratch_shapes=[pltpu.VMEM((tt + 8, tc), F32)],
        compiler_params=_cparams("parallel", "parallel", "arbitrary"),
        name="gdn_conv",
    )(qkvz, buf8, conv_w)


def _dot_b(a, b):
    return _dot(a.astype(BF16), b.astype(BF16))


def _gdn_scan_kernel(q_ref, k_ref, v_ref, z_ref, gc_ref, gr_ref, *rest, C, cps, Gv, dk, dv, rep, has_s0):
    if has_s0:
        s0_ref, gon_ref, o_ref, sN_ref, S_scr = rest
    else:
        gon_ref, o_ref, sN_ref, S_scr = rest
    t = pl.program_id(2)

    @pl.when(t == 0)
    def _():
        if has_s0:
            S_scr[...] = s0_ref[0]
        else:
            S_scr[...] = jnp.zeros_like(S_scr)

    rows = lax.broadcasted_iota(I32, (C, C), 0)
    cols = lax.broadcasted_iota(I32, (C, C), 1)
    incl = rows >= cols
    strict = rows > cols
    ltri = jnp.where(incl, 1.0, 0.0).astype(F32)
    eye = jnp.where(rows == cols, 1.0, 0.0).astype(F32)
    gon = gon_ref[...]

    chains = [(c, hh) for c in range(cps) for hh in range(Gv)]
    gcm, gcum_col, gcum_row = [], [], []
    for c in range(cps):
        r0 = c * C
        m = gc_ref[0, 0, r0:r0 + C, :]
        gcm.append(m)
        gcum_col.append(_dot_hi(ltri, m[:, Gv:]))
        gcum_row.append(lax.dot_general(gr_ref[0, 0, Gv:, r0:r0 + C], ltri, (((1,), (1,)), ((), ())),
                                        preferred_element_type=F32, precision=HI))

    st = []
    for (c, hh) in chains:
        r0 = c * C
        kh = hh // rep
        qh = q_ref[0, r0:r0 + C, kh * dk:(kh + 1) * dk].astype(F32)
        kk = k_ref[0, r0:r0 + C, kh * dk:(kh + 1) * dk].astype(F32)
        vh = v_ref[0, r0:r0 + C, hh * dv:(hh + 1) * dv].astype(F32)
        beta = gcm[c][:, hh:hh + 1]
        gcol = gcum_col[c][:, hh:hh + 1]
        grow = gcum_row[c][hh:hh + 1, :]
        glast = grow[:, C - 1:C]
        decay = jnp.exp(jnp.where(incl, gcol - grow, NEG))
        eg = jnp.exp(gcol)
        kb = kk * beta
        kq = _dot_nt(jnp.concatenate([kb, qh], axis=0).astype(BF16), kk.astype(BF16))
        p = jnp.where(strict, -(kq[:C] * decay), 0.0)
        st.append(dict(
            p=p, y=eye + p, qk=(kq[C:] * decay).astype(BF16),
            rhs=jnp.concatenate([vh * beta, kb * eg], axis=1),
            qe=qh * eg, kdT=jnp.transpose(kk * jnp.exp(glast - gcol)).astype(BF16),
            egl=jnp.exp(glast)))

    for s in st:
        s["p"] = _dot_b(s["p"], s["p"])
    m = 4
    while m < C:
        for s in st:
            r = _dot_b(jnp.concatenate([s["p"], s["y"]], axis=0), s["p"])
            s["p"] = r[:C]
            s["y"] = s["y"] + r[C:]
        m *= 2
    for s in st:
        s["y"] = s["y"] + _dot_b(s["y"], s["p"])
    for s in st:
        sol = _dot_b(s["y"], s["rhs"])
        s["u"] = sol[:, :dv]
        s["wq"] = jnp.concatenate([sol[:, dv:], s["qe"]], axis=0).astype(BF16)
        s["lhs2"] = jnp.concatenate([s["qk"], s["kdT"]], axis=0)

    S = [S_scr[hh] for hh in range(Gv)]
    for c in range(cps):
        r0 = c * C
        cur = [st[c * Gv + hh] for hh in range(Gv)]
        ws = [_dot(s["wq"], S[hh].astype(BF16)) for hh, s in enumerate(cur)]
        v_new = [(s["u"] - ws[hh][:C]).astype(BF16) for hh, s in enumerate(cur)]
        r2 = [_dot(s["lhs2"], v_new[hh]) for hh, s in enumerate(cur)]
        for hh, s in enumerate(cur):
            o = ws[hh][C:] + r2[hh][:C]
            S[hh] = S[hh] * s["egl"] + r2[hh][C:]
            zz = z_ref[0, r0:r0 + C, hh * dv:(hh + 1) * dv].astype(F32)
            o_ref[0, r0:r0 + C, hh * dv:(hh + 1) * dv] = (_rms(o, gon) * _silu(zz)).astype(o_ref.dtype)
    for hh in range(Gv):
        S_scr[hh] = S[hh]

    @pl.when(t == pl.num_programs(2) - 1)
    def _():
        sN_ref[0] = S_scr[...]


def _gdn_scan(y, qkvz, gates, s0, g_onorm, *, nk, nv, dk, dv, Gv=8):
    B, T, conv_dim = y.shape
    rep = nv // nk
    C = min(CHUNK, T)
    cps = 1 if C == T else max(1, 2 * LANES // C)
    TB = C * cps
    assert T % TB == 0 and Gv % rep == 0 and nv % Gv == 0
    Gk = Gv // rep
    HB = nv // Gv
    beta = gates[..., :nv].reshape(B, T, HB, Gv)
    g = gates[..., nv:].reshape(B, T, HB, Gv)
    gcol = jnp.transpose(jnp.concatenate([beta, g], axis=-1), (0, 2, 1, 3))
    grow = jnp.transpose(gcol, (0, 1, 3, 2))
    k_off = (nk * dk) // (Gk * dk)
    v_off = (2 * nk * dk) // (Gv * dv)
    z_off = conv_dim // (Gv * dv)
    in_specs = [pl.BlockSpec((1, TB, Gk * dk), lambda b, h, t: (b, t, h)),
                pl.BlockSpec((1, TB, Gk * dk), lambda b, h, t: (b, t, k_off + h)),
                pl.BlockSpec((1, TB, Gv * dv), lambda b, h, t: (b, t, v_off + h)),
                pl.BlockSpec((1, TB, Gv * dv), lambda b, h, t: (b, t, z_off + h)),
                pl.BlockSpec((1, 1, TB, 2 * Gv), lambda b, h, t: (b, h, t, 0)),
                pl.BlockSpec((1, 1, 2 * Gv, TB), lambda b, h, t: (b, h, 0, t))]
    args = [y, y, y, qkvz, gcol, grow]
    if s0 is not None:
        in_specs.append(pl.BlockSpec((1, Gv, dk, dv), lambda b, h, t: (b, h, 0, 0)))
        args.append(s0)
    in_specs.append(pl.BlockSpec((1, dv), lambda b, h, t: (0, 0)))
    args.append(g_onorm.reshape(1, dv).astype(F32))
    kern = functools.partial(_gdn_scan_kernel, C=C, cps=cps, Gv=Gv, dk=dk, dv=dv, rep=rep,
                             has_s0=s0 is not None)
    return pl.pallas_call(
        kern,
        grid=(B, HB, T // TB),
        in_specs=in_specs,
        out_specs=[pl.BlockSpec((1, TB, Gv * dv), lambda b, h, t: (b, t, h)),
                   pl.BlockSpec((1, Gv, dk, dv), lambda b, h, t: (b, h, 0, 0))],
        out_shape=[jax.ShapeDtypeStruct((B, T, nv * dv), BF16),
                   jax.ShapeDtypeStruct((B, nv, dk, dv), F32)],
        scratch_shapes=[pltpu.VMEM((Gv, dk, dv), F32)],
        compiler_params=_cparams("parallel", "parallel", "arbitrary"),
        name="gdn_scan",
    )(*args)


def _gdn_layer(h3, conv_buf, s0, gain, w_main, w_ba, conv_w, a_log, dt_bias, g_onorm, w_out, *, nk, nv, dk, dv):
    B, T, D = h3.shape
    M = B * T
    qk_dim = nk * dk
    conv_dim = 2 * qk_dim + nv * dv
    h2 = h3.reshape(M, D)
    qkvz = _mm(h2, [w_main], _ep_plain, [BF16], gain=gain, tn=1024, name="gdn_in_proj")[0].reshape(B, T, -1)
    pad = jnp.zeros((1, nv), F32)
    gates = _mm(h2, [w_ba], functools.partial(_gdn_gate_epilogue, nv=nv), [F32], gain=gain,
                extras=[jnp.concatenate([pad, a_log.reshape(1, nv)], axis=1),
                        jnp.concatenate([pad, dt_bias.reshape(1, nv)], axis=1)],
                name="gdn_gates")[0].reshape(B, T, 2 * nv)
    if conv_buf is None:
        buf8 = jnp.zeros((B, 8, conv_dim), F32)
    else:
        buf8 = jnp.pad(conv_buf, ((0, 0), (8 - conv_buf.shape[1], 0), (0, 0)))
    y, last8 = _gdn_conv(qkvz, buf8, conv_w, conv_dim=conv_dim, qk_dim=qk_dim, dk=dk)
    o, s_new = _gdn_scan(y, qkvz, gates, s0, g_onorm, nk=nk, nv=nv, dk=dk, dv=dv)
    h_new = _mm(o.reshape(M, nv * dv), [w_out], _ep_residual, [F32], extras=[h2], name="gdn_out_proj")[0]
    return h_new.reshape(B, T, D), last8[:, 5:, :], s_new


def _t5_bucket_np(rel):
    rel = np.asarray(rel, np.int32)
    half = NUM_BUCKETS // 2
    exact = half // 2
    base = np.where(rel > 0, half, 0)
    n = np.abs(rel)
    nf = np.maximum(n, 1).astype(np.float32)
    large = exact + (np.log(nf / np.float32(exact)) / np.float32(math.log(MAX_DISTANCE / exact))
                     * np.float32(half - exact)).astype(np.int32)
    large = np.minimum(large, half - 1)
    return base + np.where(n < exact, n, large)


def _lookup_kernel(oh_ref, tab_ref, o_ref):
    o_ref[...] = _dot_hi(oh_ref[...], tab_ref[...])


def _bias_lookup(buckets, rel_bias, far_bucket=None):
    flat = np.asarray(buckets).reshape(-1)
    nb, H = rel_bias.shape
    onehot = np.zeros((flat.size, nb), np.float32)
    onehot[np.arange(flat.size), flat] = 1.0
    if far_bucket is not None:
        onehot[:, far_bucket] -= 1.0
    R_ = flat.size
    tr = _pick_tile(R_, 2048)
    out = pl.pallas_call(
        _lookup_kernel,
        grid=(R_ // tr,),
        in_specs=[pl.BlockSpec((tr, nb), lambda i: (i, 0)), pl.BlockSpec((nb, H), lambda i: (0, 0))],
        out_specs=pl.BlockSpec((tr, H), lambda i: (i, 0)),
        out_shape=jax.ShapeDtypeStruct((R_, H), F32),
        compiler_params=_cparams("parallel"),
    )(jnp.asarray(onehot), rel_bias.astype(F32))
    return out.reshape(tuple(np.asarray(buckets).shape) + (H,))


def _sortable_key(x):
    bits = pltpu.bitcast(x, I32)
    return bits ^ ((bits >> 31) & 0x7FFFFFFF)


def _topk_threshold(keys_ref, n_tiles, tile, rows, topk, idx_bits):
    sub = tile // LANES

    def count(pred):
        def body(c, acc):
            cs = pl.multiple_of(c * tile, tile)
            for s in range(sub):
                kt = keys_ref[:, pl.ds(cs + s * LANES, LANES)]
                col = cs + s * LANES + lax.broadcasted_iota(I32, (rows, LANES), 1)
                acc = acc + pred(kt, col)
            return acc
        acc = lax.fori_loop(0, n_tiles, body, jnp.zeros((rows, LANES), F32))
        return jnp.sum(acc, axis=1, keepdims=True)

    def count_ge(cand):
        return count(lambda kt, col: jnp.where(kt >= cand, 1.0, 0.0))

    kf = float(topk)
    zero = jnp.zeros((rows, LANES), I32)
    lowest = jnp.full((rows, LANES), INT_MIN, I32)
    c_all = count_ge(lowest)
    c_pos = count_ge(zero)
    prefix = jnp.where(c_pos >= kf, zero, lowest)
    n_ge = jnp.where(c_pos >= kf, c_pos, c_all)

    def bit_cond(carry):
        it, _, n_ge = carry
        return jnp.logical_and(it < 31, jnp.max(n_ge) > kf)

    def bit_body(carry):
        it, prefix, n_ge = carry
        cand = prefix | jnp.left_shift(jnp.int32(1), 30 - it)
        cnt = count_ge(cand)
        ok = cnt >= kf
        return it + 1, jnp.where(ok, cand, prefix), jnp.where(ok, cnt, n_ge)

    _, thr, n_ge = lax.while_loop(bit_cond, bit_body, (jnp.int32(0), prefix, n_ge))

    @pl.when(jnp.max(n_ge) > kf)
    def _():
        n_gt = count(lambda kt, col: jnp.where(kt > thr, 1.0, 0.0))
        need = kf - n_gt

        def idx_body(it, p):
            cand = p | jnp.left_shift(jnp.int32(1), idx_bits - 1 - it)
            c = count(lambda kt, col: jnp.where(kt == thr, jnp.where(col < cand, 1.0, 0.0), 0.0))
            return jnp.where(c < need, cand, p)

        p = lax.fori_loop(0, idx_bits, idx_body, zero)

        def demote(c, carry):
            cs = pl.multiple_of(c * tile, tile)
            for s in range(sub):
                sl = pl.ds(cs + s * LANES, LANES)
                kt = keys_ref[:, sl]
                col = cs + s * LANES + lax.broadcasted_iota(I32, (rows, LANES), 1)
                keys_ref[:, sl] = jnp.where(kt == thr, jnp.where(col > p, kt - 1, kt), kt)
            return carry

        lax.fori_loop(0, n_tiles, demote, 0)

    return thr


def _dsa_prompt_kernel(q_ref, qi_ref, wi_ref, ki_ref, k_ref, v_ref, bias_ref, o_ref,
                       keys_ref, thr_ref, m_ref, acc_ref,
                       *, tq, ts, tf, topk, n_kv, rep, dh, idx_h, idx_d, chunk, idx_bits):
    i = pl.program_id(1)
    nvis = (i + 1) * tq
    n_st = (nvis + ts - 1) // ts
    row = lax.broadcasted_iota(I32, (tq, 1), 0)
    limit = i * tq + (row // chunk + 1) * chunk

    wi = wi_ref[0]
    qi = qi_ref[0]
    q_stack = jnp.concatenate([qi[:, h * idx_d:(h + 1) * idx_d] for h in range(idx_h)], axis=0)
    wis = [wi[:, h:h + 1] for h in range(idx_h)]

    def score_body(t, carry):
        ks = pl.multiple_of(t * ts, ts)
        kt = ki_ref[0, pl.ds(ks, ts), :]
        s_all = _dot_nt(q_stack, kt)
        sc = jnp.zeros((tq, ts), F32)
        for h in range(idx_h):
            sc = sc + wis[h] * jnp.maximum(s_all[h * tq:(h + 1) * tq], 0.0)
        sc = sc * (idx_d ** -0.5)
        col = ks + lax.broadcasted_iota(I32, (tq, ts), 1)
        sc = jnp.where(col < limit, sc, -jnp.inf)
        keys_ref[:, pl.ds(ks, ts)] = _sortable_key(sc)
        return carry

    lax.fori_loop(0, n_st, score_body, 0)

    thr_ref[...] = jnp.full((tq, LANES), INT_MIN, I32)

    @pl.when(nvis > topk)
    def _():
        if (keys_ref.shape[1] // ts) % 2 == 0:
            @pl.when(n_st % 2 == 1)
            def _():
                pad = jnp.full((tq, ts), -jnp.inf, F32)
                keys_ref[:, pl.ds(pl.multiple_of(n_st * ts, ts), ts)] = _sortable_key(pad)

            thr_ref[...] = _topk_threshold(keys_ref, (n_st + 1) // 2, 2 * ts, tq, topk, idx_bits)
        else:
            thr_ref[...] = _topk_threshold(keys_ref, n_st, ts, tq, topk, idx_bits)

    thr = thr_ref[:, :1]
    q = q_ref[0]

    qg = [jnp.concatenate([q[:, (g * rep + r) * dh:(g * rep + r + 1) * dh] for r in range(rep)], axis=0)
          for g in range(n_kv)]
    m_ref[...] = jnp.full(m_ref.shape, NEG, F32)
    acc_ref[...] = jnp.zeros(acc_ref.shape, F32)

    def attend(ks, tk, bias_of, col_lo, col_hi):
        col = ks + lax.broadcasted_iota(I32, (tq, tk), 1)
        madd = jnp.where(keys_ref[:, pl.ds(ks, tk)] >= thr, 0.0, NEG)
        if col_lo is not None:
            madd = jnp.where(col >= col_lo, madd, NEG)
        if col_hi is not None:
            madd = jnp.where(col < col_hi, madd, NEG)
        madd = jnp.concatenate([madd] * rep, axis=0)
        ones = jnp.ones((tk, dh), BF16)
        s = []
        for g in range(n_kv):
            sg = _dot_nt(qg[g], k_ref[0, pl.ds(ks, tk), g * dh:(g + 1) * dh]) + madd
            s.append(sg if bias_of is None else sg + bias_of(g))
        for g in range(n_kv):
            m_prev = m_ref[g]
            m_next = jnp.maximum(m_prev, jnp.max(s[g], axis=1, keepdims=True))
            alpha = jnp.exp2(m_prev - m_next)
            p = jnp.exp2(s[g] - jnp.concatenate([m_next] * (tk // LANES), axis=1)).astype(BF16)
            vt = jnp.concatenate([v_ref[0, pl.ds(ks, tk), g * dh:(g + 1) * dh], ones], axis=1)
            acc_ref[g] = jnp.concatenate([alpha] * (2 * dh // LANES), axis=1) * acc_ref[g] + _dot(p, vt)
            m_ref[g] = m_next

    far_end = jnp.maximum(i - 1, 0) * tq

    def far_body(t, carry):
        attend(pl.multiple_of(t * tf, tf), tf, None, None, far_end)
        return carry

    lax.fori_loop(0, (far_end + tf - 1) // tf, far_body, 0)

    @pl.when(i == 0)
    def _():
        attend(0, tq, lambda g: bias_ref[g, :, tq:], None, limit)

    @pl.when(i > 0)
    def _():
        attend(pl.multiple_of((i - 1) * tq, tq), 2 * tq, lambda g: bias_ref[g], None, limit)

    for g in range(n_kv):
        acc = acc_ref[g]
        out = acc[:, :dh] / acc[:, dh:]
        for r in range(rep):
            hcol = (g * rep + r) * dh
            o_ref[0, :, hcol:hcol + dh] = out[r * tq:(r + 1) * tq].astype(o_ref.dtype)


def _dsa_prompt(qa, wi, ki, kb, vb, rel_bias, *, n_heads, n_kv, dh, idx_h, idx_d, tq=128):
    B, T, _ = qa.shape
    assert T % tq == 0 and tq % CHUNK == 0
    topk = min(TOPK_MAX, T // 4)
    rep = n_heads // n_kv
    qd = n_heads * dh
    idd = idx_h * idx_d
    assert qd % idd == 0
    ts = _pick_tile(T, 512)
    far_bucket = int(_t5_bucket_np(np.array([-(tq + 1)]))[0])
    assert np.all(_t5_bucket_np(-np.arange(tq + 1, 4 * T)) == far_bucket)
    dq = np.arange(tq)[:, None]
    buckets = _t5_bucket_np(np.arange(2 * tq)[None, :] - tq - dq)
    bias = _bias_lookup(buckets, rel_bias, far_bucket) * LOG2E
    bias = jnp.transpose(bias.reshape(tq, 2 * tq, n_kv, rep), (2, 3, 0, 1)).reshape(n_kv, rep * tq, 2 * tq)
    kern = functools.partial(_dsa_prompt_kernel, tq=tq, ts=ts, tf=ts, topk=topk, n_kv=n_kv, rep=rep, dh=dh,
                             idx_h=idx_h, idx_d=idx_d, chunk=CHUNK, idx_bits=max(1, (T - 1).bit_length()))
    return pl.pallas_call(
        kern,
        grid=(B, T // tq),
        in_specs=[pl.BlockSpec((1, tq, qd), lambda b, i: (b, i, 0)),
                  pl.BlockSpec((1, tq, idd), lambda b, i: (b, i, qd // idd)),
                  pl.BlockSpec((1, tq, idx_h), lambda b, i: (b, i, 0)),
                  pl.BlockSpec((1, T, idx_d), lambda b, i: (b, 0, 0)),
                  pl.BlockSpec((1, T, n_kv * dh), lambda b, i: (b, 0, 0)),
                  pl.BlockSpec((1, T, n_kv * dh), lambda b, i: (b, 0, 0)),
                  pl.BlockSpec((n_kv, rep * tq, 2 * tq), lambda b, i: (0, 0, 0))],
        out_specs=pl.BlockSpec((1, tq, qd), lambda b, i: (b, i, 0)),
        out_shape=jax.ShapeDtypeStruct((B, T, qd), BF16),
        scratch_shapes=[pltpu.VMEM((tq, T), I32), pltpu.VMEM((tq, LANES), I32),
                        pltpu.VMEM((n_kv, rep * tq, LANES), F32), pltpu.VMEM((n_kv, rep * tq, 2 * dh), F32)],
        compiler_params=_cparams("parallel", "arbitrary"),
        name="dsa_prompt",
    )(qa, qa, wi, ki, kb, vb, bias)


def _dsa_sample_kernel(q_ref, qi_ref, wi_ref, ki_ref, k_ref, v_ref, bias_ref, o_ref, keys_ref,
                       *, tq, S, n_valid, topk, n_kv, rep, dh, idx_h, idx_d, idx_bits):
    wi = wi_ref[0]
    qi = qi_ref[0]
    kt = ki_ref[0]
    sc = jnp.zeros((tq, S), F32)
    for h in range(idx_h):
        sc = sc + wi[:, h:h + 1] * jnp.maximum(_dot_nt(qi[:, h * idx_d:(h + 1) * idx_d], kt), 0.0)
    sc = sc * (idx_d ** -0.5)
    col = lax.broadcasted_iota(I32, (tq, S), 1)
    valid = col < n_valid
    keys_ref[...] = _sortable_key(jnp.where(valid, sc, -jnp.inf))
    thr = _topk_threshold(keys_ref, S // LANES, LANES, tq, topk, idx_bits)[:, :1]
    madd = jnp.where(keys_ref[...] >= thr, jnp.where(valid, 0.0, NEG), NEG)
    madd = jnp.concatenate([madd] * rep, axis=0)
    q = q_ref[0]
    for g in range(n_kv):
        qg = jnp.concatenate([q[:, (g * rep + r) * dh:(g * rep + r + 1) * dh] for r in range(rep)], axis=0)
        s = _dot_nt(qg, k_ref[0, :, g * dh:(g + 1) * dh]) + bias_ref[g] + madd
        p = jnp.exp2(s - jnp.max(s, axis=1, keepdims=True))
        out = _dot(p.astype(BF16), v_ref[0, :, g * dh:(g + 1) * dh]) / jnp.sum(p, axis=1, keepdims=True)
        for r in range(rep):
            hcol = (g * rep + r) * dh
            o_ref[0, :, hcol:hcol + dh] = out[r * tq:(r + 1) * tq].astype(o_ref.dtype)


def _dsa_sample(qa, wi, ki_all, k_all, v_all, rel_bias, *, past, n_heads, n_kv, dh, idx_h, idx_d):
    B, tq, _ = qa.shape
    S = k_all.shape[1]
    n_valid = past + tq
    topk = min(TOPK_MAX, n_valid // 4)
    rep = n_heads // n_kv
    qd = n_heads * dh
    idd = idx_h * idx_d
    rel = np.minimum(np.arange(S), n_valid - 1)[None, :] - (past + np.arange(tq))[:, None]
    bias = _bias_lookup(_t5_bucket_np(rel), rel_bias) * LOG2E
    bias = jnp.transpose(bias.reshape(tq, S, n_kv, rep), (2, 3, 0, 1)).reshape(n_kv, rep * tq, S)
    kern = functools.partial(_dsa_sample_kernel, tq=tq, S=S, n_valid=n_valid, topk=topk, n_kv=n_kv, rep=rep,
                             dh=dh, idx_h=idx_h, idx_d=idx_d, idx_bits=max(1, (S - 1).bit_length()))
    return pl.pallas_call(
        kern,
        grid=(B,),
        in_specs=[pl.BlockSpec((1, tq, qd), lambda b: (b, 0, 0)),
                  pl.BlockSpec((1, tq, idd), lambda b: (b, 0, qd // idd)),
                  pl.BlockSpec((1, tq, idx_h), lambda b: (b, 0, 0)),
                  pl.BlockSpec((1, S, idx_d), lambda b: (b, 0, 0)),
                  pl.BlockSpec((1, S, n_kv * dh), lambda b: (b, 0, 0)),
                  pl.BlockSpec((1, S, n_kv * dh), lambda b: (b, 0, 0)),
                  pl.BlockSpec((n_kv, rep * tq, S), lambda b: (0, 0, 0))],
        out_specs=pl.BlockSpec((1, tq, qd), lambda b: (b, 0, 0)),
        out_shape=jax.ShapeDtypeStruct((B, tq, qd), BF16),
        scratch_shapes=[pltpu.VMEM((tq, S), I32)],
        compiler_params=_cparams("parallel"),
        name="dsa_sample",
    )(qa, qa, wi, ki_all, k_all, v_all, bias)


def _dsa_tail_epilogue(accs, extras, j, *, idx_d, idx_h):
    a = accs[0]
    ki = _rms(a[:, :idx_d], extras[0][:, :idx_d])
    return [jnp.concatenate([ki, a[:, idx_d:] * idx_h ** -0.5], axis=1)]


def _dsa_layer(h3, cache, gain, w_qa, w_kv, w_tail, g_q, g_k, g_ki, w_out, rel_bias,
               *, n_heads, n_kv, dh, idx_h, idx_d):
    B, T, D = h3.shape
    M = B * T
    qd = n_heads * dh
    kvd = n_kv * dh
    h2 = h3.reshape(M, D)
    tn = 512
    g_qa = jnp.concatenate([jnp.tile(g_q, n_heads), jnp.ones((idx_h * idx_d,), F32)]).reshape(1, -1)
    qa = _mm(h2, [w_qa], functools.partial(_ep_headnorm, dh=dh, n_norm_tiles=qd // tn,
                                           post_scale=dh ** -0.5 * LOG2E), [BF16],
             gain=gain, extras=[g_qa], tn=tn, name="dsa_q_proj")[0].reshape(B, T, -1)
    g_kv = jnp.concatenate([jnp.tile(g_k, n_kv), jnp.ones((kvd,), F32)]).reshape(1, -1)
    kv = _mm(h2, [w_kv], functools.partial(_ep_headnorm, dh=dh, n_norm_tiles=kvd // tn), [F32],
             gain=gain, extras=[g_kv], tn=tn, name="dsa_kv_proj")[0]
    g_tail = jnp.concatenate([g_ki, jnp.ones((idx_h,), F32)]).reshape(1, -1)
    tail = _mm(h2, [w_tail], functools.partial(_dsa_tail_epilogue, idx_d=idx_d, idx_h=idx_h), [F32],
               gain=gain, extras=[g_tail], name="dsa_idx_proj")[0]
    k_new = kv[:, :kvd].reshape(B, T, kvd)
    v_new = kv[:, kvd:].reshape(B, T, kvd)
    ki_new = tail[:, :idx_d].reshape(B, T, idx_d)
    wi = tail[:, idx_d:].reshape(B, T, idx_h)
    dims = dict(n_heads=n_heads, n_kv=n_kv, dh=dh, idx_h=idx_h, idx_d=idx_d)
    if cache is None:
        att = _dsa_prompt(qa, wi, ki_new.astype(BF16), k_new.astype(BF16), v_new.astype(BF16), rel_bias, **dims)
    else:
        ck, cv, cki = cache
        past = ck.shape[1]
        S = -(-(past + T) // LANES) * LANES
        padr = ((0, 0), (0, S - past - T), (0, 0))
        k_all = jnp.pad(jnp.concatenate([ck.reshape(B, past, kvd), k_new], axis=1).astype(BF16), padr)
        v_all = jnp.pad(jnp.concatenate([cv.reshape(B, past, kvd), v_new], axis=1).astype(BF16), padr)
        ki_all = jnp.pad(jnp.concatenate([cki, ki_new], axis=1).astype(BF16), padr)
        att = _dsa_sample(qa, wi, ki_all, k_all, v_all, rel_bias, past=past, **dims)
    h_new = _mm(att.reshape(M, qd), [w_out], _ep_residual, [F32], extras=[h2], name="dsa_out_proj")[0]
    return (h_new.reshape(B, T, D), k_new.reshape(B, T, n_kv, dh), v_new.reshape(B, T, n_kv, dh), ki_new)


def _router_kernel(h_ref, g_ref, wr_ref, xn_ref, ro_ref, *, n_exp):
    x = _rms(h_ref[...], g_ref[...])
    xn_ref[...] = x.astype(BF16)
    logits = _dot_hi(x, wr_ref[...])
    lane = lax.broadcasted_iota(I32, logits.shape, 1).astype(F32)
    lg = jnp.where(lane < n_exp, logits, -jnp.inf)
    m1 = jnp.max(lg, axis=1, keepdims=True)
    i1 = jnp.min(jnp.where(lg == m1, lane, float(LANES)), axis=1, keepdims=True)
    lg2 = jnp.where(lane == i1, -jnp.inf, lg)
    m2 = jnp.max(lg2, axis=1, keepdims=True)
    i2 = jnp.min(jnp.where(lg2 == m2, lane, float(LANES)), axis=1, keepdims=True)
    e = jnp.exp(m2 - m1)
    g1 = 1.0 / (1.0 + e)
    g2 = e / (1.0 + e)
    ro_ref[...] = jnp.where(lane == 0, i1, jnp.where(lane == 1, i2, jnp.where(lane == 2, g1,
                            jnp.where(lane == 3, g2, 0.0))))


def _router(h2, gain, w_router, *, tm=512):
    M, D = h2.shape
    n_exp = w_router.shape[1]
    tm = _pick_tile(M, tm)
    wr = jnp.pad(w_router.astype(F32), ((0, 0), (0, LANES - n_exp)))
    return pl.pallas_call(
        functools.partial(_router_kernel, n_exp=n_exp),
        grid=(M // tm,),
        in_specs=[pl.BlockSpec((tm, D), lambda i: (i, 0)), pl.BlockSpec((1, D), lambda i: (0, 0)),
                  pl.BlockSpec((D, LANES), lambda i: (0, 0))],
        out_specs=[pl.BlockSpec((tm, D), lambda i: (i, 0)), pl.BlockSpec((tm, LANES), lambda i: (i, 0))],
        out_shape=[jax.ShapeDtypeStruct((M, D), BF16), jax.ShapeDtypeStruct((M, LANES), F32)],
        compiler_params=_cparams("parallel"),
        name="moe_router",
    )(h2, gain.reshape(1, D).astype(F32), wr)


def _moe(h_list, gain, w_router, wg, wu, wd, *, bm=512):
    n_exp = w_router.shape[1]
    routed = [_router(h, gain, w_router) for h in h_list]
    xn = jnp.concatenate([r[0] for r in routed], axis=0)
    ro = jnp.concatenate([r[1] for r in routed], axis=0)
    N, D = xn.shape
    A = N * MOE_TOP_K
    flat_e = ro[:, :MOE_TOP_K].astype(I32).reshape(-1)
    onehot = (flat_e[:, None] == jnp.arange(n_exp, dtype=I32)[None, :]).astype(I32)
    csum = jnp.cumsum(onehot, axis=0)
    rank = jnp.take_along_axis(csum, flat_e[:, None], axis=1)[:, 0] - 1
    counts = csum[-1]
    padded = (counts + bm - 1) // bm * bm
    pad_end = jnp.cumsum(padded)
    dest = (pad_end - padded)[flat_e] + rank
    n_blocks = (A + n_exp * (bm - 1)) // bm
    row_src = jnp.zeros((n_blocks * bm,), I32).at[dest].set(jnp.arange(A, dtype=I32) // MOE_TOP_K)
    block_e = jnp.minimum(jnp.searchsorted(pad_end, jnp.arange(n_blocks, dtype=I32) * bm, side='right'),
                          n_exp - 1).astype(I32)
    n_used = (pad_end[-1:] // bm).astype(I32)
    xs = xn.at[row_src].get(mode="promise_in_bounds")
    yp = _ffn_experts(xs, block_e, n_used, wg, wu, wd, bm=bm)
    dest2 = dest.reshape(N, MOE_TOP_K)
    y0 = yp.at[dest2[:, 0]].get(mode="promise_in_bounds")
    y1 = yp.at[dest2[:, 1]].get(mode="promise_in_bounds")
    return y0, y1, ro


def kernel(x_prompt, x_sample, state_gdn, state_gdn_conv, cache_dsa_k, cache_dsa_v, cache_dsa_kidx, p_prompt, p_sample, g_mix, g_ffn, g_ple, w_ple_in, w_ple_gate, w_in_gdn, conv_w_gdn, a_log_gdn, dt_bias_gdn, g_onorm_gdn, w_out_gdn, w_in_dsa, g_qnorm_dsa, g_knorm_dsa, g_kidx_norm_dsa, w_out_dsa, rel_bias, w_gate_ffn, w_up_ffn, w_down_ffn, w_router, w_gate_moe, w_up_moe, w_down_moe):
    depth = g_mix.shape[0]
    Bp, Tp, D = x_prompt.shape
    Bs, Ts, _ = x_sample.shape
    nv = a_log_gdn.shape[1]
    dk, dv = state_gdn.shape[3], state_gdn.shape[4]
    conv_dim = conv_w_gdn.shape[2]
    nk = (conv_dim - nv * dv) // (2 * dk)
    gdn_dims = dict(nk=nk, nv=nv, dk=dk, dv=dv)
    n_heads = rel_bias.shape[1]
    n_kv, dh = cache_dsa_k.shape[3], cache_dsa_k.shape[4]
    idx_d = cache_dsa_kidx.shape[3]
    qd, kvd = n_heads * dh, n_kv * dh
    idx_h = (w_in_dsa.shape[2] - qd - 2 * kvd - idx_d) // (idx_d + 1)
    dsa_dims = dict(n_heads=n_heads, n_kv=n_kv, dh=dh, idx_h=idx_h, idx_d=idx_d)

    hp, hs = x_prompt, x_sample
    outs = {k: [] for k in ("sp", "cp", "ss", "cs", "kp", "vp", "kip", "ks", "vs", "kis")}
    for i in range(depth):
        j = i // 2
        if i % 2 == 0:
            w_in = w_in_gdn[j]
            n_main = conv_dim + nv * dv
            wa = (g_mix[i], w_in[:, :n_main].astype(BF16), w_in[:, n_main:].astype(BF16), conv_w_gdn[j],
                  a_log_gdn[j], dt_bias_gdn[j], g_onorm_gdn[j], w_out_gdn[j].astype(BF16))
            hp, cbp, sp = _gdn_layer(hp, None, None, *wa, **gdn_dims)
            hs, cbs, ss = _gdn_layer(hs, state_gdn_conv[j], state_gdn[j], *wa, **gdn_dims)
            outs["sp"].append(sp); outs["cp"].append(cbp); outs["ss"].append(ss); outs["cs"].append(cbs)
            wf = (g_ffn[i], w_gate_ffn[j].astype(BF16), w_up_ffn[j].astype(BF16), w_down_ffn[j].astype(BF16))
            hp = _ffn_dense(hp.reshape(Bp * Tp, D), *wf).reshape(Bp, Tp, D)
            hs = _ffn_dense(hs.reshape(Bs * Ts, D), *wf).reshape(Bs, Ts, D)
            moe_p = moe_s = None
        else:
            w_in = w_in_dsa[j]
            o1 = qd + 2 * kvd
            o2 = o1 + idx_h * idx_d
            w_qa = jnp.concatenate([w_in[:, :qd], w_in[:, o1:o2]], axis=1).astype(BF16)
            wb = (g_mix[i], w_qa, w_in[:, qd:o1].astype(BF16), w_in[:, o2:].astype(BF16), g_qnorm_dsa[j],
                  g_knorm_dsa[j], g_kidx_norm_dsa[j], w_out_dsa[j].astype(BF16), rel_bias)
            hp, k, v, ki = _dsa_layer(hp, None, *wb, **dsa_dims)
            outs["kp"].append(k); outs["vp"].append(v); outs["kip"].append(ki)
            hs, k, v, ki = _dsa_layer(hs, (cache_dsa_k[j], cache_dsa_v[j], cache_dsa_kidx[j]), *wb, **dsa_dims)
            outs["ks"].append(k); outs["vs"].append(v); outs["kis"].append(ki)
            y0, y1, ro = _moe([hp.reshape(Bp * Tp, D), hs.reshape(Bs * Ts, D)], g_ffn[i], w_router[j],
                              w_gate_moe[j].astype(BF16), w_up_moe[j].astype(BF16), w_down_moe[j].astype(BF16))
            moe_p, moe_s = (y0, y1, ro, 0), (y0, y1, ro, Bp * Tp)
        wpl = (g_ple[i], w_ple_in[i].astype(BF16), w_ple_gate[i].astype(BF16))
        hp = _ple_add(hp.reshape(Bp * Tp, D), p_prompt[i].reshape(Bp * Tp, -1), *wpl, moe=moe_p).reshape(Bp, Tp, D)
        hs = _ple_add(hs.reshape(Bs * Ts, D), p_sample[i].reshape(Bs * Ts, -1), *wpl, moe=moe_s).reshape(Bs, Ts, D)
    st = lambda name: jnp.stack(outs[name])
    return (hp, hs, st("sp"), st("cp"), st("kp"), st("vp"), st("kip"),
            st("ss"), st("cs"), st("ks"), st("vs"), st("kis"))
```
